```python
import math
import jax, jax.numpy as jnp
from jax import lax
import numpy as np

D_MODEL = 4096
BATCH = 4
SEQ = 2048
DEPTH = 2
DEC_BATCH = 128
DEC_SEQ = 4
PAST_LEN = 16384
PAGE_SIZE = 128

FOX_HD = 64
FOX_HEADS = D_MODEL // (4 * FOX_HD)
MLA_VHD = 128
MLA_HEADS = D_MODEL // (2 * MLA_VHD)
Q_LORA = 768
KV_LORA = 256
QK_NOPE = 128
QK_ROPE = 32
ROPE_THETA = 10000.0
NSA_HD = 64
NSA_HEADS = D_MODEL // (4 * NSA_HD)
CMP_L = 32
CMP_D = 16
SLC_B = 64
N_SEL = 16
WINDOW = 512
N_BRANCH = 3
NUM_BUCKETS = 32
MAX_DIST = 128
D_FF = 2 * D_MODEL
CONV_W = 3
Q_BLOCK = 128
EPS = 1e-6
NEG = -1e30
BIG = 1e30
IN_SIZES = (FOX_HEADS * FOX_HD, FOX_HD, FOX_HD, FOX_HEADS,
            Q_LORA, KV_LORA, QK_ROPE,
            NSA_HEADS * NSA_HD, 2 * NSA_HD, 2 * NSA_HD, 2 * NSA_HD, N_BRANCH * NSA_HEADS)
P_IN = sum(IN_SIZES)

kernel_name = 'hymba_fox_mla_nsa_convffn_step'


def rmsnorm(x, g):
    x32 = x.astype(jnp.float32)
    y = x32 * lax.rsqrt(jnp.mean(x32 * x32, axis=-1, keepdims=True) + EPS)
    return (y * g.astype(jnp.float32)).astype(x.dtype)


def split_cols(p, sizes):
    return jnp.split(p, np.cumsum(sizes)[:-1].tolist(), axis=-1)


def rope(x, pos):
    half = x.shape[-1] // 2
    freq = ROPE_THETA ** (-jnp.arange(half, dtype=jnp.float32) / half)
    ang = pos.astype(jnp.float32)[:, None] * freq
    ang = ang.reshape((ang.shape[0],) + (1,) * (x.ndim - 3) + (half,))
    cos, sin = jnp.cos(ang).astype(x.dtype), jnp.sin(ang).astype(x.dtype)
    x1, x2 = x[..., :half], x[..., half:]
    return jnp.concatenate([x1 * cos - x2 * sin, x2 * cos + x1 * sin], axis=-1)


def rel_bias(table, dist):
    n = jnp.maximum(dist, 0)
    max_exact = NUM_BUCKETS // 2
    large = max_exact + (jnp.log(jnp.maximum(n, 1).astype(jnp.float32) / max_exact)
                         / math.log(MAX_DIST / max_exact) * (NUM_BUCKETS - max_exact)).astype(jnp.int32)
    bucket = jnp.where(n < max_exact, n, jnp.minimum(large, NUM_BUCKETS - 1))
    return table[bucket].astype(jnp.float32)


def masked_softmax(logits, mask):
    p = jax.nn.softmax(jnp.where(mask, logits, NEG), axis=-1)
    return jnp.where(mask, p, 0.0)


def sweep_query_blocks(fn, q_args, q_pos):
    tq = q_pos.shape[0]
    qb = math.gcd(tq, Q_BLOCK)
    nb = tq // qb
    def split(a):
        return jnp.moveaxis(a.reshape((a.shape[0], nb, qb) + a.shape[2:]), 1, 0)
    blocks = tuple(split(a) for a in q_args) + (q_pos.reshape(nb, qb),)
    out = jnp.moveaxis(lax.map(lambda blk: fn(*blk), blocks), 0, 1)
    return out.reshape((out.shape[0], tq) + out.shape[3:])


def gather_pages(pool, l, page_table):
    rows = pool[l, page_table]
    return rows.reshape((rows.shape[0], rows.shape[1] * rows.shape[2]) + rows.shape[3:])


def take_rows(rows, pos):
    b = pos.shape[0]
    flat = jnp.clip(pos.reshape(b, -1), 0, rows.shape[1] - 1)
    out = rows[jnp.arange(b)[:, None], flat]
    return out.reshape(pos.shape + rows.shape[2:])


def take_paged_rows(pool, l, page_table, new_rows, pos):
    b = pos.shape[0]
    past_len = page_table.shape[1] * PAGE_SIZE
    flat = pos.reshape(b, -1)
    pc = jnp.clip(flat, 0, past_len - 1)
    page = page_table[jnp.arange(b)[:, None], pc // PAGE_SIZE]
    old = pool[l, page, pc % PAGE_SIZE]
    new = take_rows(new_rows, flat - past_len)
    out = jnp.where((flat < past_len)[:, :, None, None], old, new)
    return out.reshape(pos.shape + pool.shape[3:])


def overlap_matrix(nc, ns):
    i = np.arange(nc)[:, None] * CMP_D
    j = np.arange(ns)[None, :] * SLC_B
    ov = np.minimum(i + CMP_L, j + SLC_B) - np.maximum(i, j)
    return (np.maximum(ov, 0) / CMP_L).astype(np.float32)


def fox_attention(q, kv, cum_q, cum_k, q_pos, k_pos):
    k, v = kv[:, :, 0], kv[:, :, 1]
    cum_kt = jnp.moveaxis(cum_k, 1, 2)[:, :, None, :]
    scale = FOX_HD ** -0.5
    def block(q_b, cq_b, pos_b):
        s = jnp.einsum('bqhd,bkd->bhqk', q_b, k).astype(jnp.float32) * scale
        s = s + jnp.moveaxis(cq_b, 1, 2)[..., None] - cum_kt
        p = masked_softmax(s, k_pos[None, :] <= pos_b[:, None])
        return jnp.einsum('bhqk,bkd->bqhd', p.astype(v.dtype), v)
    return sweep_query_blocks(block, (q, cum_q), q_pos)


def mla_attention(q_lat, q_rope, ckv, krope, q_pos, k_pos):
    scale = (QK_NOPE + QK_ROPE) ** -0.5
    def block(ql_b, qr_b, pos_b):
        s = (jnp.einsum('bqhc,bkc->bhqk', ql_b, ckv)
             + jnp.einsum('bqhr,bkr->bhqk', qr_b, krope)).astype(jnp.float32) * scale
        p = masked_softmax(s, k_pos[None, :] <= pos_b[:, None])
        return jnp.einsum('bhqk,bkc->bqhc', p.astype(ckv.dtype), ckv)
    return sweep_query_blocks(block, (q_lat, q_rope), q_pos)


def nsa_attention(q, gates, q_pos, kv_cmp_full, gather_slc, kv_win, win_pos0, w_cmp_l, table):
    b, t = kv_cmp_full.shape[:2]
    scale = NSA_HD ** -0.5
    ratio = CMP_L // CMP_D
    nh = t // CMP_D
    nc = nh - ratio + 1
    halves = kv_cmp_full[:, :nh * CMP_D].reshape(b, nh, CMP_D, 2, NSA_HD)
    w = w_cmp_l.reshape(2, ratio, CMP_D, NSA_HD)
    kv_c = sum(jnp.einsum('bnjed,ejd->bned', halves[:, r:r + nc], w[:, r]) for r in range(ratio))
    k_c, v_c = kv_c[:, :, 0], kv_c[:, :, 1]
    cmp_end = jnp.arange(nc, dtype=jnp.int32) * CMP_D + CMP_L - 1
    ns = -(-t // SLC_B)
    n_sel = min(N_SEL, ns)
    overlap = jnp.asarray(overlap_matrix(nc, ns))
    blk = jnp.arange(ns, dtype=jnp.int32)
    win_pad = jnp.pad(kv_win, ((0, 0), (WINDOW, 0), (0, 0), (0, 0)))

    def block(q_b, g_b, pos_b):
        qb = pos_b.shape[0]
        dt = q_b.dtype
        s_c = jnp.einsum('bqhd,bnd->bhqn', q_b, k_c).astype(jnp.float32) * scale
        s_c = s_c + jnp.moveaxis(rel_bias(table, pos_b[:, None] - cmp_end[None, :]), -1, 0)
        p_c = masked_softmax(s_c, cmp_end[None, :] <= pos_b[:, None])
        o_c = jnp.einsum('bhqn,bnd->bqhd', p_c.astype(dt), v_c)
        imp = jnp.einsum('bhqn,ns->bqs', p_c, overlap)
        cur = pos_b[:, None] // SLC_B
        forced = (blk[None, :] == 0) | (blk[None, :] == cur) | (blk[None, :] == cur - 1)
        valid = blk[None, :] * SLC_B <= pos_b[:, None]
        score = jnp.where(forced, BIG, jnp.where(valid, imp, -BIG))
        _, sel = lax.top_k(score, n_sel)
        pos_s = (sel[..., None] * SLC_B + jnp.arange(SLC_B, dtype=jnp.int32)).reshape(q_b.shape[0], qb, n_sel * SLC_B)
        kv_s = gather_slc(pos_s)
        dist_s = pos_b[None, :, None] - pos_s
        s_s = jnp.einsum('bqhd,bqkd->bhqk', q_b, kv_s[..., 0, :]).astype(jnp.float32) * scale
        s_s = s_s + jnp.moveaxis(rel_bias(table, dist_s), -1, 1)
        p_s = masked_softmax(s_s, (dist_s >= 0)[:, None])
        o_s = jnp.einsum('bhqk,bqkd->bqhd', p_s.astype(dt), kv_s[..., 1, :])
        start = pos_b[0] - win_pos0
        kv_w = lax.dynamic_slice_in_dim(win_pad, start, WINDOW + qb, axis=1)
        kpos = pos_b[0] - WINDOW + jnp.arange(WINDOW + qb, dtype=jnp.int32)
        dist_w = pos_b[:, None] - kpos[None, :]
        mask_w = (dist_w >= 0) & (dist_w < WINDOW) & (kpos >= win_pos0)[None, :]
        s_w = jnp.einsum('bqhd,bkd->bhqk', q_b, kv_w[..., 0, :]).astype(jnp.float32) * scale
        s_w = s_w + jnp.moveaxis(rel_bias(table, dist_w), -1, 0)
        p_w = masked_softmax(s_w, mask_w)
        o_w = jnp.einsum('bhqk,bkd->bqhd', p_w.astype(dt), kv_w[..., 1, :])
        g = jax.nn.sigmoid(g_b.astype(jnp.float32))
        out = (g[..., 0:1] * o_c.astype(jnp.float32) + g[..., 1:2] * o_s.astype(jnp.float32)
               + g[..., 2:3] * o_w.astype(jnp.float32))
        return out.astype(dt)
    return sweep_query_blocks(block, (q, gates), q_pos)


def mixing_layer(h, q_pos, l, w_in, b_fgate, g_qa, wq_b, g_kva, wkv_b, w_cmp, rel_bias_table, w_buf, past):
    b, t, _ = h.shape
    fq, fk, fv, ff, cq, ckv_raw, kr_raw, nq, ncmp, nslc, nwin, ngate = split_cols(h @ w_in[l], IN_SIZES)
    fq = fq.reshape(b, t, FOX_HEADS, FOX_HD)
    fox_kv = jnp.stack([fk, fv], axis=2)
    fox_logf = jax.nn.log_sigmoid(ff.astype(jnp.float32) + b_fgate[l].astype(jnp.float32)).astype(h.dtype)
    q_mla = jnp.einsum('btc,chd->bthd', rmsnorm(cq, g_qa[l]), wq_b[l])
    q_rope = rope(q_mla[..., QK_NOPE:], q_pos)
    ckv = rmsnorm(ckv_raw, g_kva[l])
    krope = rope(kr_raw, q_pos)
    w_uk, w_uv = wkv_b[l][..., :QK_NOPE], wkv_b[l][..., QK_NOPE:]
    q_lat = jnp.einsum('bthd,chd->bthc', q_mla[..., :QK_NOPE], w_uk)
    nq = nq.reshape(b, t, NSA_HEADS, NSA_HD)
    ncmp = ncmp.reshape(b, t, 2, NSA_HD)
    nslc = nslc.reshape(b, t, 2, NSA_HD)
    nwin = nwin.reshape(b, t, 2, NSA_HD)
    ngate = ngate.reshape(b, t, NSA_HEADS, N_BRANCH)
    if past is None:
        k_pos = q_pos
        fkv_all, logf_all, ckv_all, kr_all, cmp_all = fox_kv, fox_logf, ckv, krope, ncmp
        gather_slc = lambda pos: take_rows(nslc, pos)
        kv_win, win_pos0 = nwin, 0
        new_win = jnp.pad(nwin, ((0, 0), (w_buf, 0), (0, 0), (0, 0)))[:, -w_buf:]
    else:
        page_table, fox_kv_pool, fox_logf_pool, ckv_pool, kr_pool, cmp_pool, slc_pool, win_state = past
        past_len = page_table.shape[1] * PAGE_SIZE
        k_pos = jnp.arange(past_len + t, dtype=jnp.int32)
        fkv_all = jnp.concatenate([gather_pages(fox_kv_pool, l, page_table), fox_kv], axis=1)
        logf_all = jnp.concatenate([gather_pages(fox_logf_pool, l, page_table), fox_logf], axis=1)
        ckv_all = jnp.concatenate([gather_pages(ckv_pool, l, page_table), ckv], axis=1)
        kr_all = jnp.concatenate([gather_pages(kr_pool, l, page_table), krope], axis=1)
        cmp_all = jnp.concatenate([gather_pages(cmp_pool, l, page_table), ncmp], axis=1)
        gather_slc = lambda pos: take_paged_rows(slc_pool, l, page_table, nslc, pos)
        kv_win = jnp.concatenate([win_state[l], nwin], axis=1)
        win_pos0 = past_len - w_buf
        new_win = kv_win[:, -w_buf:]
    cum = jnp.cumsum(logf_all.astype(jnp.float32), axis=1)
    o_fox = fox_attention(fq, fkv_all, cum[:, -t:], cum, q_pos, k_pos)
    o_mla = jnp.einsum('bthc,chd->bthd', mla_attention(q_lat, q_rope, ckv_all, kr_all, q_pos, k_pos), w_uv)
    o_nsa = nsa_attention(nq, ngate, q_pos, cmp_all, gather_slc, kv_win, win_pos0, w_cmp[l], rel_bias_table)
    mix = jnp.concatenate([o_fox.reshape(b, t, -1), o_mla.reshape(b, t, -1), o_nsa.reshape(b, t, -1)], axis=-1)
    return mix, (fox_kv, fox_logf, ckv, krope, ncmp, nslc, new_win)


def conv_ffn(h, prev, w_gate, w_up, w_conv, b_conv, w_down):
    t = h.shape[1]
    g = jnp.concatenate([prev, h @ w_gate], axis=1)
    c = b_conv + sum(g[:, k:k + t] * w_conv[k] for k in range(CONV_W))
    y = (jax.nn.silu(c) * (h @ w_up)) @ w_down
    return y, g[:, t:]


def setup_inputs(seed: int = 0) -> dict:
    key = jax.random.key(seed)
    ks = jax.random.split(key, 32)
    def nrm(i, shape, scale=1.0):
        return jax.random.normal(ks[i], shape, jnp.float32) * scale
    n_pages = PAST_LEN // PAGE_SIZE
    n_used = DEC_BATCH * n_pages
    n_phys = n_used + n_used // 4
    w_buf = min(WINDOW, PAST_LEN)
    pool = (DEPTH, n_phys, PAGE_SIZE)
    page_table = jax.random.permutation(ks[10], n_phys)[:n_used].reshape(DEC_BATCH, n_pages).astype(jnp.int32)
    return {
        'x_prompt': nrm(0, (BATCH, SEQ, D_MODEL)),
        'x_sample': nrm(1, (DEC_BATCH, DEC_SEQ, D_MODEL)),
        'cache_fox_kv': nrm(2, pool + (2, FOX_HD)),
        'cache_fox_logf': jax.nn.log_sigmoid(1.5 + nrm(3, pool + (FOX_HEADS,))),
        'cache_mla_ckv': nrm(4, pool + (KV_LORA,)),
        'cache_mla_krope': nrm(5, pool + (QK_ROPE,)),
        'cache_nsa_cmp_kv': nrm(6, pool + (2, NSA_HD)),
        'cache_nsa_slc_kv': nrm(7, pool + (2, NSA_HD)),
        'state_nsa_win_kv': nrm(8, (DEPTH, DEC_BATCH, w_buf, 2, NSA_HD)),
        'state_conv': nrm(9, (DEPTH, DEC_BATCH, CONV_W - 1, D_FF)),
        'page_table': page_table,
        'g_attn': 1.0 + nrm(11, (DEPTH, D_MODEL), 0.02),
        'w_in': nrm(12, (DEPTH, D_MODEL, P_IN), D_MODEL ** -0.5),
        'b_fgate': 1.5 + nrm(13, (DEPTH, FOX_HEADS), 0.1),
        'g_qa': 1.0 + nrm(14, (DEPTH, Q_LORA), 0.02),
        'wq_b': nrm(15, (DEPTH, Q_LORA, MLA_HEADS, QK_NOPE + QK_ROPE), Q_LORA ** -0.5),
        'g_kva': 1.0 + nrm(16, (DEPTH, KV_LORA), 0.02),
        'wkv_b': nrm(17, (DEPTH, KV_LORA, MLA_HEADS, QK_NOPE + MLA_VHD), KV_LORA ** -0.5),
        'w_cmp': (1.0 + nrm(18, (DEPTH, 2, CMP_L, NSA_HD), 0.1)) / CMP_L,
        'rel_bias_table': nrm(19, (NUM_BUCKETS, NSA_HEADS), 0.5),
        'w_out': nrm(20, (DEPTH, D_MODEL, D_MODEL), D_MODEL ** -0.5),
        'g_ffn': 1.0 + nrm(21, (DEPTH, D_MODEL), 0.02),
        'w_gate': nrm(22, (DEPTH, D_MODEL, D_FF), D_MODEL ** -0.5),
        'w_up': nrm(23, (DEPTH, D_MODEL, D_FF), D_MODEL ** -0.5),
        'w_conv': nrm(24, (DEPTH, CONV_W, D_FF), CONV_W ** -0.5),
        'b_conv': nrm(25, (DEPTH, D_FF), 0.02),
        'w_down': nrm(26, (DEPTH, D_FF, D_MODEL), D_FF ** -0.5),
        'g_final': 1.0 + nrm(27, (D_MODEL,), 0.02),
    }


def reference(x_prompt, x_sample, cache_fox_kv, cache_fox_logf, cache_mla_ckv, cache_mla_krope,
              cache_nsa_cmp_kv, cache_nsa_slc_kv, state_nsa_win_kv, state_conv, page_table,
              g_attn, w_in, b_fgate, g_qa, wq_b, g_kva, wkv_b, w_cmp, rel_bias_table, w_out,
              g_ffn, w_gate, w_up, w_conv, b_conv, w_down, g_final):
    w_buf = state_nsa_win_kv.shape[2]
    past_len = page_table.shape[1] * PAGE_SIZE
    pos_p = jnp.arange(x_prompt.shape[1], dtype=jnp.int32)
    pos_s = past_len + jnp.arange(x_sample.shape[1], dtype=jnp.int32)
    past = (page_table, cache_fox_kv, cache_fox_logf, cache_mla_ckv, cache_mla_krope,
            cache_nsa_cmp_kv, cache_nsa_slc_kv, state_nsa_win_kv)
    xp, xs = x_prompt, x_sample
    states_p, states_s = [], []
    for l in range(DEPTH):
        mp, st_p = mixing_layer(rmsnorm(xp, g_attn[l]), pos_p, l, w_in, b_fgate, g_qa, wq_b, g_kva, wkv_b,
                                w_cmp, rel_bias_table, w_buf, None)
        xp = xp + mp @ w_out[l]
        ms, st_s = mixing_layer(rmsnorm(xs, g_attn[l]), pos_s, l, w_in, b_fgate, g_qa, wq_b, g_kva, wkv_b,
                                w_cmp, rel_bias_table, w_buf, past)
        xs = xs + ms @ w_out[l]
        fp, cnv_p = conv_ffn(rmsnorm(xp, g_ffn[l]), jnp.zeros((xp.shape[0], CONV_W - 1, D_FF), xp.dtype),
                             w_gate[l], w_up[l], w_conv[l], b_conv[l], w_down[l])
        xp = xp + fp
        fs, cnv_s = conv_ffn(rmsnorm(xs, g_ffn[l]), state_conv[l],
                             w_gate[l], w_up[l], w_conv[l], b_conv[l], w_down[l])
        xs = xs + fs
        states_p.append(st_p + (cnv_p,))
        states_s.append(st_s + (cnv_s,))
    fox_kv_p, fox_logf_p, ckv_p, krope_p, cmp_kv_p, slc_kv_p, win_kv_p, conv_p = [jnp.stack(z) for z in zip(*states_p)]
    fox_kv_s, fox_logf_s, ckv_s, krope_s, cmp_kv_s, slc_kv_s, win_kv_s, conv_s = [jnp.stack(z) for z in zip(*states_s)]
    y_prompt = rmsnorm(xp, g_final)
    y_sample = rmsnorm(xs, g_final)
    return (y_prompt, y_sample, fox_kv_p, fox_kv_s, fox_logf_p, fox_logf_s, ckv_p, ckv_s, krope_p, krope_s,
            cmp_kv_p, cmp_kv_s, slc_kv_p, slc_kv_s, win_kv_p, win_kv_s, conv_p, conv_s)
```

```python
import functools
import math

import numpy as np
import jax
import jax.numpy as jnp
from jax import lax
from jax.experimental import pallas as pl
from jax.experimental.pallas import tpu as pltpu

F32 = jnp.float32
BF16 = jnp.bfloat16

PAGE_SIZE = 128
FOX_HD = 64
MLA_VHD = 128
QK_NOPE = 128
QK_ROPE = 32
ROPE_THETA = 10000.0
NSA_HD = 64
CMP_L = 32
CMP_D = 16
SLC_B = 64
N_SEL = 16
WINDOW = 512
N_BRANCH = 3
NUM_BUCKETS = 32
MAX_DIST = 128
CONV_W = 3
EPS = 1e-6
NEG = -1e30
BIG = 1e30

LANE = 128
VMEM_LIMIT = 56 * 1024 * 1024
TQ = 128


def _cparams(sem):
    return pltpu.CompilerParams(dimension_semantics=sem, vmem_limit_bytes=VMEM_LIMIT)


def _row_tile(n, cap):
    best = None
    for t in range(8, min(n, cap) + 1, 8):
        if n % t == 0:
            best = t
    return n if best is None else best


def _split3(x):
    a = x.astype(BF16)
    r = x - a.astype(F32)
    b = r.astype(BF16)
    c = (r - b.astype(F32)).astype(BF16)
    return a, b, c


def _dot(a, b):
    return jnp.dot(a, b, preferred_element_type=F32)


def _dot_nt(a, b):
    return lax.dot_general(a, b, (((1,), (1,)), ((), ())), preferred_element_type=F32)


def _sel_dot(sel, x):
    s = sel.astype(BF16)
    a, b, c = _split3(x)
    return _dot(s, a) + _dot(s, b) + _dot(s, c)


def _dot_sel(x, sel):
    s = sel.astype(BF16)
    a, b, c = _split3(x)
    return _dot(a, s) + _dot(b, s) + _dot(c, s)


def _iota(shape, dim):
    return lax.broadcasted_iota(jnp.int32, shape, dim)


def _rms_kernel(x_ref, g_ref, o_ref):
    x = x_ref[...].astype(F32)
    y = x * lax.rsqrt(jnp.mean(x * x, axis=-1, keepdims=True) + EPS)
    o_ref[...] = (y * g_ref[...].astype(F32)).astype(o_ref.dtype)


def rmsnorm_rows(x, g, out_dtype):
    n, d = x.shape
    tm = _row_tile(n, 512)
    return pl.pallas_call(
        _rms_kernel,
        grid=(n // tm,),
        in_specs=[pl.BlockSpec((tm, d), lambda i: (i, 0)), pl.BlockSpec((1, d), lambda i: (0, 0))],
        out_specs=pl.BlockSpec((tm, d), lambda i: (i, 0)),
        out_shape=jax.ShapeDtypeStruct((n, d), out_dtype),
        compiler_params=_cparams(("parallel",)),
        name="rmsnorm",
    )(x, g.reshape(1, d))


def _mm_kernel(*refs, nk, has_res):
    if has_res:
        a_ref, w_ref, r_ref, o_ref = refs[:4]
        scratch = refs[4:]
    else:
        a_ref, w_ref, o_ref = refs[:3]
        r_ref = None
        scratch = refs[3:]
    part = _dot(a_ref[...].astype(BF16), w_ref[...])
    if nk == 1:
        if has_res:
            part = part + r_ref[...]
        o_ref[...] = part.astype(o_ref.dtype)
        return
    acc_ref, = scratch
    k = pl.program_id(2)

    @pl.when(k == 0)
    def _():
        acc_ref[...] = part

    @pl.when(k > 0)
    def _():
        acc_ref[...] += part

    @pl.when(k == nk - 1)
    def _():
        out = acc_ref[...]
        if has_res:
            out = out + r_ref[...]
        o_ref[...] = out.astype(o_ref.dtype)


def matmul(a, w, res=None, out_dtype=F32, tm_cap=1088, tn_cap=512, tk_cap=4096):
    n, kdim = a.shape
    m = w.shape[1]
    tm = _row_tile(n, tm_cap)
    tn = max(t for t in range(LANE, min(m, tn_cap) + 1, LANE) if m % t == 0)
    tk = kdim if kdim <= tk_cap else max(t for t in range(LANE, tk_cap + 1, LANE) if kdim % t == 0)
    nk = kdim // tk
    in_specs = [pl.BlockSpec((tm, tk), lambda i, j, k: (i, k)),
                pl.BlockSpec((tk, tn), lambda i, j, k: (k, j))]
    args = [a, w]
    if res is not None:
        in_specs.append(pl.BlockSpec((tm, tn), lambda i, j, k: (i, j)))
        args.append(res)
    return pl.pallas_call(
        functools.partial(_mm_kernel, nk=nk, has_res=res is not None),
        grid=(n // tm, m // tn, nk),
        in_specs=in_specs,
        out_specs=pl.BlockSpec((tm, tn), lambda i, j, k: (i, j)),
        out_shape=jax.ShapeDtypeStruct((n, m), out_dtype),
        scratch_shapes=[pltpu.VMEM((tm, tn), F32)] if nk > 1 else [],
        compiler_params=_cparams(("parallel", "parallel", "arbitrary")),
        name="matmul",
    )(*args)


def _headmm_kernel(a_ref, w_ref, o_ref):
    o_ref[...] = _dot(a_ref[...].astype(BF16), w_ref[0]).astype(o_ref.dtype)


def head_matmul(a, w, out_dtype):
    n = a.shape[0]
    h, ka, kb = w.shape
    tm = _row_tile(n, 1088)
    return pl.pallas_call(
        _headmm_kernel,
        grid=(n // tm, h),
        in_specs=[pl.BlockSpec((tm, ka), lambda i, j: (i, j)),
                  pl.BlockSpec((1, ka, kb), lambda i, j: (j, 0, 0))],
        out_specs=pl.BlockSpec((tm, kb), lambda i, j: (i, j)),
        out_shape=jax.ShapeDtypeStruct((n, h * kb), out_dtype),
        compiler_params=_cparams(("parallel", "parallel")),
        name="head_matmul",
    )(a, w)


def _pad_to(n, m):
    return -(-n // m) * m


def _in_layout(h, q_lora, kv_lora):
    sizes = [("fq", h * FOX_HD), ("fkv", 2 * FOX_HD), ("ff", h), ("cq", q_lora), ("ckv", kv_lora),
             ("kr", QK_ROPE), ("nq", h * NSA_HD), ("ncmp", 2 * NSA_HD), ("nslc", 2 * NSA_HD),
             ("nwin", 2 * NSA_HD), ("ngate", N_BRANCH * h)]
    off, lay = 0, {}
    for name, sz in sizes:
        lay[name] = (off, sz)
        off += _pad_to(sz, LANE)
    lay["total"] = _pad_to(off, 512)
    return lay


def _pad_w_in(w, lay):
    d = w.shape[0]
    order = ["fq", "fkv", "ff", "cq", "ckv", "kr", "nq", "ncmp", "nslc", "nwin", "ngate"]
    cols, src, pos = [], 0, 0
    for name in order:
        off, sz = lay[name]
        if off > pos:
            cols.append(jnp.zeros((d, off - pos), w.dtype))
        cols.append(w[:, src:src + sz])
        src += sz
        pos = off + sz
    assert src == w.shape[1], (src, w.shape)
    if lay["total"] > pos:
        cols.append(jnp.zeros((d, lay["total"] - pos), w.dtype))
    return jnp.concatenate(cols, axis=1).astype(BF16)


def _rope_lanes(x, cosf, sins, half):
    n = x.shape[-1]
    lane = _iota(x.shape, x.ndim - 1)
    fwd = pltpu.roll(x, n - half, x.ndim - 1)
    bwd = pltpu.roll(x, half, x.ndim - 1)
    partner = jnp.where(lane % (2 * half) < half, fwd, bwd)
    return x * cosf + partner * sins


def _post_kernel(p_ref, bf_ref, gq_ref, gkv_ref, cos_ref, sin_ref,
                 logf_ref, cqn_ref, ckv_ref, kr_ref, *, lay):
    o, s = lay["ff"]
    ff = p_ref[:, o:o + LANE]
    z = ff + bf_ref[...]
    logf = jnp.minimum(z, 0.0) - jnp.log1p(jnp.exp(-jnp.abs(z)))
    logf_ref[...] = logf[:, :s]
    o, s = lay["cq"]
    x = p_ref[:, o:o + s]
    y = x * lax.rsqrt(jnp.mean(x * x, axis=-1, keepdims=True) + EPS)
    cqn_ref[...] = (y * gq_ref[...]).astype(cqn_ref.dtype)
    o, s = lay["ckv"]
    x = p_ref[:, o:o + s]
    y = x * lax.rsqrt(jnp.mean(x * x, axis=-1, keepdims=True) + EPS)
    ckv_ref[...] = y * gkv_ref[...]
    o, s = lay["kr"]
    x = p_ref[:, o:o + LANE]
    kr_ref[...] = _rope_lanes(x, cos_ref[...], sin_ref[...], QK_ROPE // 2)[:, :s]


def post_proj(p, b_fgate, g_qa, g_kva, cos_k, sin_k, lay):
    n, wtot = p.shape
    h = lay["ff"][1]
    q_lora, kv_lora = lay["cq"][1], lay["ckv"][1]
    tm = _row_tile(n, 512)
    bf = jnp.zeros((1, LANE), F32).at[0, :h].set(b_fgate.astype(F32))
    row = lambda w: pl.BlockSpec((tm, w), lambda i: (i, 0))
    full = lambda w: pl.BlockSpec((1, w), lambda i: (0, 0))
    return pl.pallas_call(
        functools.partial(_post_kernel, lay=lay),
        grid=(n // tm,),
        in_specs=[row(wtot), full(LANE), full(q_lora), full(kv_lora), row(LANE), row(LANE)],
        out_specs=[row(h), row(q_lora), row(kv_lora), row(QK_ROPE)],
        out_shape=[jax.ShapeDtypeStruct((n, h), F32), jax.ShapeDtypeStruct((n, q_lora), BF16),
                   jax.ShapeDtypeStruct((n, kv_lora), F32), jax.ShapeDtypeStruct((n, QK_ROPE), F32)],
        compiler_params=_cparams(("parallel",)),
        name="post_proj",
    )(p, bf, g_qa.reshape(1, -1).astype(F32), g_kva.reshape(1, -1).astype(F32), cos_k, sin_k)


def _qrope_kernel(x_ref, cos_ref, sin_ref, o_ref):
    o_ref[...] = _rope_lanes(x_ref[...], cos_ref[...], sin_ref[...], QK_ROPE // 2)


def q_rope_rows(q_mla, nope_w, cos_q, sin_q):
    n = q_mla.shape[0]
    w = q_mla.shape[1] - nope_w
    assert nope_w % w == 0
    tm = _row_tile(n, 1088)
    return pl.pallas_call(
        _qrope_kernel,
        grid=(n // tm,),
        in_specs=[pl.BlockSpec((tm, w), lambda i: (i, nope_w // w)),
                  pl.BlockSpec((tm, w), lambda i: (i, 0)), pl.BlockSpec((tm, w), lambda i: (i, 0))],
        out_specs=pl.BlockSpec((tm, w), lambda i: (i, 0)),
        out_shape=jax.ShapeDtypeStruct((n, w), F32),
        compiler_params=_cparams(("parallel",)),
        name="q_rope",
    )(q_mla, cos_q, sin_q)


def _rope_tables(pos, groups):
    half = QK_ROPE // 2
    freq = ROPE_THETA ** (-jnp.arange(half, dtype=F32) / half)
    ang = pos.astype(F32)[:, None] * freq
    cos, sin = jnp.cos(ang), jnp.sin(ang)
    cosf = jnp.tile(jnp.concatenate([cos, cos], axis=1), (1, groups))
    sins = jnp.tile(jnp.concatenate([-sin, sin], axis=1), (1, groups))
    return cosf, sins


def _flash_init(m_ref, l_ref, acc_ref):
    m_ref[...] = jnp.full(m_ref.shape, NEG, F32)
    l_ref[...] = jnp.zeros(l_ref.shape, F32)
    acc_ref[...] = jnp.zeros(acc_ref.shape, F32)


def _flash_update(s, v, m_ref, l_ref, acc_ref, h, mask=None, zero_masked=False):
    if mask is not None:
        s = jnp.where(mask, s, NEG)
    m_old = m_ref[h]
    m_new = jnp.maximum(m_old, jnp.max(s, axis=-1, keepdims=True))
    alpha = jnp.exp(m_old - m_new)
    p = jnp.exp(s - m_new)
    if zero_masked:
        p = jnp.where(mask, p, 0.0)
    l_ref[h] = alpha * l_ref[h] + jnp.sum(p, axis=-1, keepdims=True)
    acc_ref[h] = alpha * acc_ref[h] + _dot(p.astype(BF16), v)
    m_ref[h] = m_new


def _flash_out(l_ref, acc_ref, h):
    l = l_ref[h]
    return acc_ref[h] * (1.0 / jnp.where(l == 0.0, 1.0, l))


def _place(rows, cols, offset):
    return (_iota((rows, cols), 1) == _iota((rows, cols), 0) + offset).astype(BF16)


def _head_block(q, h, hd):
    per = LANE // hd
    blk = q[:, (h // per) * LANE:(h // per + 1) * LANE]
    sh = (h % per) * hd
    if sh:
        blk = pltpu.roll(blk, LANE - sh, 1)
    return jnp.where(_iota(blk.shape, 1) < hd, blk, 0.0)


def _fox_prompt_kernel(q_ref, kv_ref, logf_ref, o_ref, kcat, cum, qs, m_ref, l_ref, acc_ref,
                       *, nh, t, tk):
    qi = pl.program_id(1)
    tq = TQ
    scale = FOX_HD ** -0.5

    @pl.when(qi == 0)
    def _build():
        kcat[:, :LANE] = kv_ref[0].astype(BF16)
        low = (_iota((tk, tk), 1) <= _iota((tk, tk), 0)).astype(BF16)
        places = [_place(nh, LANE, j * nh) for j in range(3)]

        def body(i, carry):
            r0 = pl.multiple_of(i * tk, tk)
            c = _sel_dot(low, logf_ref[0, pl.ds(r0, tk), :]) + carry
            cum[pl.ds(r0, tk), :] = c
            parts = _split3(-c)
            kcat[pl.ds(r0, tk), LANE:] = sum(_dot(p, pm) for p, pm in zip(parts, places)).astype(BF16)
            return c[tk - 1:tk, :]

        lax.fori_loop(0, t // tk, body, jnp.zeros((1, nh), F32))

    q = q_ref[0]
    lane = _iota((tq, LANE), 1)
    for h in range(nh):
        qh = _head_block(q, h, FOX_HD) * scale
        eh = jnp.where((lane < 3 * nh) & (lane % nh == h), 1.0, 0.0)
        qs[h] = jnp.concatenate([qh, eh], axis=1).astype(BF16)
    _flash_init(m_ref, l_ref, acc_ref)
    cq = cum[pl.ds(pl.multiple_of(qi * tq, tq), tq), :]
    qpos = qi * tq + _iota((tq, tk), 0)

    def step(kt, masked):
        r0 = pl.multiple_of(kt * tk, tk)
        kc = kcat[pl.ds(r0, tk), :]
        v = kc[:, :LANE]
        mask = (r0 + _iota((tq, tk), 1) <= qpos) if masked else None
        for h in range(nh):
            s = _dot_nt(qs[h], kc) + cq[:, h:h + 1]
            _flash_update(s, v, m_ref, l_ref, acc_ref, h, mask)

    n_kt = ((qi + 1) * tq + tk - 1) // tk

    def loop_body(kt, carry):
        step(kt, False)
        return carry

    lax.fori_loop(0, n_kt - 1, loop_body, 0)
    step(n_kt - 1, True)
    for h in range(nh):
        o_ref[0, :, h * FOX_HD:(h + 1) * FOX_HD] = _flash_out(l_ref, acc_ref, h)[:, FOX_HD:].astype(o_ref.dtype)


def fox_prompt(fq, fkv, logf):
    b, t, _ = fq.shape
    nh = logf.shape[-1]
    tk = min(256, t)
    return pl.pallas_call(
        functools.partial(_fox_prompt_kernel, nh=nh, t=t, tk=tk),
        grid=(b, t // TQ),
        in_specs=[pl.BlockSpec((1, TQ, nh * FOX_HD), lambda i, j: (i, j, 0)),
                  pl.BlockSpec((1, t, 2 * FOX_HD), lambda i, j: (i, 0, 0)),
                  pl.BlockSpec((1, t, nh), lambda i, j: (i, 0, 0))],
        out_specs=pl.BlockSpec((1, TQ, nh * FOX_HD), lambda i, j: (i, j, 0)),
        out_shape=jax.ShapeDtypeStruct((b, t, nh * FOX_HD), BF16),
        scratch_shapes=[pltpu.VMEM((t, 2 * LANE), BF16), pltpu.VMEM((t, nh), F32),
                        pltpu.VMEM((nh, TQ, 2 * LANE), BF16), pltpu.VMEM((nh, TQ, 1), F32),
                        pltpu.VMEM((nh, TQ, 1), F32), pltpu.VMEM((nh, TQ, LANE), F32)],
        compiler_params=_cparams(("arbitrary", "arbitrary")),
        name="fox_prompt",
    )(fq, fkv, logf)


def _mla_prompt_kernel(ql_ref, qr_ref, ckv_ref, kr_ref, o_ref, kcat, qs, m_ref, l_ref, acc_ref,
                       *, nh, c, tk):
    qi = pl.program_id(1)
    tq = TQ
    scale = (QK_NOPE + QK_ROPE) ** -0.5

    @pl.when(qi == 0)
    def _build():
        kcat[:, :c] = ckv_ref[0].astype(BF16)
        kcat[:, c:] = _dot(kr_ref[0].astype(BF16), _place(QK_ROPE, LANE, 0)).astype(BF16)

    qr = (qr_ref[0] * scale).astype(BF16)
    for h in range(nh):
        sel = (_iota((nh * QK_ROPE, LANE), 0) == _iota((nh * QK_ROPE, LANE), 1) + h * QK_ROPE).astype(BF16)
        qs[h, :, :c] = (ql_ref[0, :, h * c:(h + 1) * c] * scale).astype(BF16)
        qs[h, :, c:] = _dot(qr, sel).astype(BF16)
    _flash_init(m_ref, l_ref, acc_ref)
    qpos = qi * tq + _iota((tq, tk), 0)

    def step(kt, masked):
        r0 = pl.multiple_of(kt * tk, tk)
        kc = kcat[pl.ds(r0, tk), :]
        v = kc[:, :c]
        mask = (r0 + _iota((tq, tk), 1) <= qpos) if masked else None
        for h in range(nh):
            _flash_update(_dot_nt(qs[h], kc), v, m_ref, l_ref, acc_ref, h, mask)

    n_kt = ((qi + 1) * tq + tk - 1) // tk

    def loop_body(kt, carry):
        step(kt, False)
        return carry

    lax.fori_loop(0, n_kt - 1, loop_body, 0)
    step(n_kt - 1, True)
    for h in range(nh):
        o_ref[0, :, h * c:(h + 1) * c] = _flash_out(l_ref, acc_ref, h).astype(o_ref.dtype)


def mla_prompt(q_lat, q_rope, ckv, krope):
    b, t, c = ckv.shape
    nh = q_lat.shape[-1] // c
    tk = min(256, t)
    return pl.pallas_call(
        functools.partial(_mla_prompt_kernel, nh=nh, c=c, tk=tk),
        grid=(b, t // TQ),
        in_specs=[pl.BlockSpec((1, TQ, nh * c), lambda i, j: (i, j, 0)),
                  pl.BlockSpec((1, TQ, nh * QK_ROPE), lambda i, j: (i, j, 0)),
                  pl.BlockSpec((1, t, c), lambda i, j: (i, 0, 0)),
                  pl.BlockSpec((1, t, QK_ROPE), lambda i, j: (i, 0, 0))],
        out_specs=pl.BlockSpec((1, TQ, nh * c), lambda i, j: (i, j, 0)),
        out_shape=jax.ShapeDtypeStruct((b, t, nh * c), BF16),
        scratch_shapes=[pltpu.VMEM((t, c + LANE), BF16), pltpu.VMEM((nh, TQ, c + LANE), BF16),
                        pltpu.VMEM((nh, TQ, 1), F32), pltpu.VMEM((nh, TQ, 1), F32),
                        pltpu.VMEM((nh, TQ, c), F32)],
        compiler_params=_cparams(("arbitrary", "arbitrary")),
        name="mla_prompt",
    )(q_lat, q_rope, ckv, krope)


NEW_PAD = 16


def _pages_dma(pool_ref, layer, pt_ref, b, page0, npages, buf, slot, sem, start):
    def body(j, c):
        pg = pt_ref[b, page0 + j]
        cp = pltpu.make_async_copy(
            pool_ref.at[layer, pg],
            buf.at[slot, pl.ds(pl.multiple_of(j * PAGE_SIZE, PAGE_SIZE), PAGE_SIZE)],
            sem.at[slot])
        if start:
            cp.start()
        else:
            cp.wait()
        return c
    lax.fori_loop(0, npages, body, 0)


def _paged_pipeline(g, n_steps, issue):
    slot = g % 2

    @pl.when(g == 0)
    def _():
        issue(g, slot, True)

    @pl.when(g + 1 < n_steps)
    def _():
        issue(g + 1, 1 - slot, True)

    issue(g, slot, False)
    return slot


def _fox_sample_kernel(pt_ref, q_ref, kvn_ref, lfn_ref, kvpool, lfpool, o_ref,
                       kvbuf, lfbuf, sems, qcat, cnq, cneg_new, carry, m_ref, l_ref, acc_ref,
                       *, layer, nh, ts, nchunk, cpages, tk, n_steps):
    g = pl.program_id(0)
    b = g // nchunk
    c = g % nchunk
    r = ts * nh
    scale = FOX_HD ** -0.5

    def issue(step, slot, start):
        sb = step // nchunk
        p0 = (nchunk - 1 - step % nchunk) * cpages
        _pages_dma(kvpool, layer, pt_ref, sb, p0, cpages, kvbuf, slot, sems.at[0], start)
        _pages_dma(lfpool, layer, pt_ref, sb, p0, cpages, lfbuf, slot, sems.at[1], start)

    slot = _paged_pipeline(g, n_steps, issue)
    places = [_place(nh, LANE, j * nh) for j in range(3)]

    @pl.when(c == 0)
    def _prep():
        row = _iota((r, LANE), 0)
        lane = _iota((r, LANE), 1)
        qp = _dot((q_ref[0] * scale).astype(BF16), _place(FOX_HD, LANE, 0))
        eh = jnp.where((lane < 3 * nh) & (lane % nh == row % nh), 1.0, 0.0)
        qcat[...] = jnp.concatenate([qp, eh], axis=1).astype(BF16)
        low = (_iota((NEW_PAD, NEW_PAD), 1) <= _iota((NEW_PAD, NEW_PAD), 0)).astype(BF16)
        cn = _sel_dot(low, lfn_ref[0])
        pick = (_iota((r, NEW_PAD), 1) == _iota((r, NEW_PAD), 0) // nh).astype(BF16)
        gq = _sel_dot(pick, cn)
        hsel = _iota((r, nh), 1) == _iota((r, nh), 0) % nh
        cnq[...] = jnp.sum(jnp.where(hsel, gq, 0.0), axis=1, keepdims=True)
        cneg_new[...] = sum(_dot(p, pm) for p, pm in zip(_split3(-cn), places)).astype(BF16)
        carry[...] = jnp.zeros(carry.shape, F32)
        _flash_init(m_ref, l_ref, acc_ref)

    upper = (_iota((tk, tk), 1) > _iota((tk, tk), 0)).astype(BF16)
    ntile = cpages * PAGE_SIZE // tk

    def tile(i, cy):
        r0 = pl.multiple_of((ntile - 1 - i) * tk, tk)
        x = lfbuf[slot, pl.ds(r0, tk), :]
        suf = _sel_dot(upper, x) + carry[...]
        carry[...] = suf[0:1, :] + x[0:1, :]
        kb = sum(_dot(p, pm) for p, pm in zip(_split3(suf), places)).astype(BF16)
        kv = kvbuf[slot, pl.ds(r0, tk), :].astype(BF16)
        s = _dot_nt(qcat[...], jnp.concatenate([kv, kb], axis=1)) + cnq[...]
        _flash_update(s, kv, m_ref, l_ref, acc_ref, 0)
        return cy

    lax.fori_loop(0, ntile, tile, 0)

    @pl.when(c == nchunk - 1)
    def _fin():
        kvn = kvn_ref[0].astype(BF16)
        s = _dot_nt(qcat[...], jnp.concatenate([kvn, cneg_new[...]], axis=1)) + cnq[...]
        qi = _iota((r, NEW_PAD), 0) // nh
        kj = _iota((r, NEW_PAD), 1)
        _flash_update(s, kvn, m_ref, l_ref, acc_ref, 0, (kj <= qi) & (kj < ts))
        o_ref[0] = _flash_out(l_ref, acc_ref, 0).astype(o_ref.dtype)


def _paged_call(kern, n_steps, in_blocks, pools, out_block, out_shape, scratch, name):
    grid_spec = pltpu.PrefetchScalarGridSpec(
        num_scalar_prefetch=1,
        grid=(n_steps,),
        in_specs=in_blocks + [pl.BlockSpec(memory_space=pl.ANY)] * pools,
        out_specs=out_block,
        scratch_shapes=scratch)
    return pl.pallas_call(kern, grid_spec=grid_spec, out_shape=out_shape,
                          compiler_params=_cparams(("arbitrary",)), name=name)


def _pad_new(x):
    return jnp.pad(x, ((0, 0), (0, NEW_PAD - x.shape[1]), (0, 0)))


def fox_sample(layer, page_table, fq, fkv_new, logf_new, kv_pool, logf_pool):
    bs, r, _ = fq.shape
    ts = fkv_new.shape[1]
    nh = r // ts
    npages = page_table.shape[1]
    nchunk = 2 if npages % 2 == 0 and npages >= 16 else 1
    cpages = npages // nchunk
    tk = min(256, cpages * PAGE_SIZE)
    n_steps = bs * nchunk
    per_b = lambda w: pl.BlockSpec((1,) + w, lambda g, pt: (g // nchunk, 0, 0))
    kern = functools.partial(_fox_sample_kernel, layer=layer, nh=nh, ts=ts, nchunk=nchunk,
                             cpages=cpages, tk=tk, n_steps=n_steps)
    rows = cpages * PAGE_SIZE
    scratch = [pltpu.VMEM((2, rows, LANE), F32), pltpu.VMEM((2, rows, nh), F32),
               pltpu.SemaphoreType.DMA((2, 2)),
               pltpu.VMEM((r, 2 * LANE), BF16), pltpu.VMEM((r, 1), F32), pltpu.VMEM((NEW_PAD, LANE), BF16),
               pltpu.VMEM((1, nh), F32), pltpu.VMEM((1, r, 1), F32), pltpu.VMEM((1, r, 1), F32),
               pltpu.VMEM((1, r, LANE), F32)]
    return _paged_call(
        kern, n_steps, [per_b((r, FOX_HD)), per_b((NEW_PAD, LANE)), per_b((NEW_PAD, nh))], 2,
        per_b((r, LANE)), jax.ShapeDtypeStruct((bs, r, LANE), BF16), scratch, "fox_sample",
    )(page_table, fq, _pad_new(fkv_new), _pad_new(logf_new), kv_pool, logf_pool)


def _mla_sample_kernel(pt_ref, ql_ref, qr_ref, ckvn_ref, krn_ref, ckvpool, krpool, o_ref,
                       ckvbuf, krbuf, sems, qcat, m_ref, l_ref, acc_ref,
                       *, layer, nh, ts, nchunk, cpages, tk, n_steps):
    g = pl.program_id(0)
    c = g % nchunk
    r = ts * nh
    scale = (QK_NOPE + QK_ROPE) ** -0.5

    def issue(step, slot, start):
        sb = step // nchunk
        p0 = (step % nchunk) * cpages
        _pages_dma(ckvpool, layer, pt_ref, sb, p0, cpages, ckvbuf, slot, sems.at[0], start)
        _pages_dma(krpool, layer, pt_ref, sb, p0, cpages, krbuf, slot, sems.at[1], start)

    slot = _paged_pipeline(g, n_steps, issue)
    place = _place(QK_ROPE, LANE, 0)

    @pl.when(c == 0)
    def _prep():
        ql = (ql_ref[0] * scale).astype(BF16)
        qr = _dot((qr_ref[0] * scale).astype(BF16), place).astype(BF16)
        qcat[...] = jnp.concatenate([ql, qr], axis=1)
        _flash_init(m_ref, l_ref, acc_ref)

    def kcat(ckv, kr):
        ckv = ckv.astype(BF16)
        return jnp.concatenate([ckv, _dot(kr.astype(BF16), place).astype(BF16)], axis=1), ckv

    def tile(i, cy):
        r0 = pl.multiple_of(i * tk, tk)
        kc, v = kcat(ckvbuf[slot, pl.ds(r0, tk), :], krbuf[slot, pl.ds(r0, tk), :])
        _flash_update(_dot_nt(qcat[...], kc), v, m_ref, l_ref, acc_ref, 0)
        return cy

    lax.fori_loop(0, cpages * PAGE_SIZE // tk, tile, 0)

    @pl.when(c == nchunk - 1)
    def _fin():
        kc, v = kcat(ckvn_ref[0], krn_ref[0])
        qi = _iota((r, NEW_PAD), 0) // nh
        kj = _iota((r, NEW_PAD), 1)
        _flash_update(_dot_nt(qcat[...], kc), v, m_ref, l_ref, acc_ref, 0, (kj <= qi) & (kj < ts))
        o_ref[0] = _flash_out(l_ref, acc_ref, 0).astype(o_ref.dtype)


def mla_sample(layer, page_table, q_lat, q_rope, ckv_new, kr_new, ckv_pool, kr_pool):
    bs, r, c = q_lat.shape
    ts = ckv_new.shape[1]
    nh = r // ts
    npages = page_table.shape[1]
    nchunk = 2 if npages % 2 == 0 and npages >= 16 else 1
    cpages = npages // nchunk
    tk = min(256, cpages * PAGE_SIZE)
    n_steps = bs * nchunk
    per_b = lambda w: pl.BlockSpec((1,) + w, lambda g, pt: (g // nchunk, 0, 0))
    kern = functools.partial(_mla_sample_kernel, layer=layer, nh=nh, ts=ts, nchunk=nchunk,
                             cpages=cpages, tk=tk, n_steps=n_steps)
    rows = cpages * PAGE_SIZE
    scratch = [pltpu.VMEM((2, rows, c), F32), pltpu.VMEM((2, rows, QK_ROPE), F32),
               pltpu.SemaphoreType.DMA((2, 2)),
               pltpu.VMEM((r, c + LANE), BF16), pltpu.VMEM((1, r, 1), F32), pltpu.VMEM((1, r, 1), F32),
               pltpu.VMEM((1, r, c), F32)]
    return _paged_call(
        kern, n_steps,
        [per_b((r, c)), per_b((r, QK_ROPE)), per_b((NEW_PAD, c)), per_b((NEW_PAD, QK_ROPE))], 2,
        per_b((r, c)), jax.ShapeDtypeStruct((bs, r, c), BF16), scratch, "mla_sample",
    )(page_table, q_lat, q_rope, _pad_new(ckv_new), _pad_new(kr_new), ckv_pool, kr_pool)


def _bucket_np(dist):
    n = np.maximum(dist, 0)
    max_exact = NUM_BUCKETS // 2
    large = max_exact + (np.log(np.maximum(n, 1).astype(np.float32) / max_exact)
                         / math.log(MAX_DIST / max_exact) * (NUM_BUCKETS - max_exact)).astype(np.int32)
    return np.where(n < max_exact, n, np.minimum(large, NUM_BUCKETS - 1)).astype(np.int32)


def _bias_kernel(tab_ref, bk_ref, o_ref):
    h = pl.program_id(0)
    bk = bk_ref[...]
    acc = jnp.zeros(bk.shape, F32)
    for b in range(NUM_BUCKETS):
        acc = jnp.where(bk == b, tab_ref[b, h], acc)
    o_ref[0] = acc


def bias_lookup(table, bucket):
    nh = table.shape[1]
    m, n = bucket.shape
    tm = _row_tile(m, 256) if m % 8 == 0 else m
    return pl.pallas_call(
        _bias_kernel,
        grid=(nh, m // tm),
        in_specs=[pl.BlockSpec(memory_space=pltpu.SMEM), pl.BlockSpec((tm, n), lambda i, j: (j, 0))],
        out_specs=pl.BlockSpec((1, tm, n), lambda i, j: (i, j, 0)),
        out_shape=jax.ShapeDtypeStruct((nh, m, n), F32),
        compiler_params=_cparams(("parallel", "parallel")),
        name="bias_lookup",
    )(table.astype(F32), jnp.asarray(bucket))


def _overlap_np(ncp, nsp, nc, ns):
    i = np.arange(ncp)[:, None] * CMP_D
    j = np.arange(nsp)[None, :] * SLC_B
    ov = np.maximum(np.minimum(i + CMP_L, j + SLC_B) - np.maximum(i, j), 0) / CMP_L
    ov = np.where((np.arange(ncp)[:, None] < nc) & (np.arange(nsp)[None, :] < ns), ov, 0.0)
    return ov.astype(np.float32)


def _pool_page(x, wcmp, h1, h2, row0):
    per = PAGE_SIZE // CMP_D
    g = (_iota((per, PAGE_SIZE), 1) // CMP_D == _iota((per, PAGE_SIZE), 0)).astype(BF16)
    wa = jnp.concatenate([wcmp[0:CMP_D]] * per, axis=0)
    wb = jnp.concatenate([wcmp[CMP_D:CMP_L]] * per, axis=0)
    h1[pl.ds(row0, per), :] = _sel_dot(g, x * wa)
    h2[pl.ds(row0, per), :] = _sel_dot(g, x * wb)


def _cmp_softmax(s, mask):
    s = jnp.where(mask, s, NEG)
    m = jnp.max(s, axis=-1, keepdims=True)
    p = jnp.where(mask, jnp.exp(s - m), 0.0)
    l = jnp.sum(p, axis=-1, keepdims=True)
    return p * (1.0 / jnp.where(l == 0.0, 1.0, l))


def _nsa_cmp_prompt_kernel(q_ref, x_ref, w_ref, bias_ref, ov_ref, oc_ref, imp_ref, kvc, h1, h2,
                           *, nh, t, nc, ncp):
    qi = pl.program_id(1)
    tq = TQ
    scale = NSA_HD ** -0.5
    per = PAGE_SIZE // CMP_D

    @pl.when(qi == 0)
    def _pool():
        h1[...] = jnp.zeros(h1.shape, F32)
        h2[...] = jnp.zeros(h2.shape, F32)
        w = w_ref[...]

        def body(c, cy):
            r0 = pl.multiple_of(c * PAGE_SIZE, PAGE_SIZE)
            _pool_page(x_ref[0, pl.ds(r0, PAGE_SIZE), :], w, h1, h2, pl.multiple_of(c * per, per))
            return cy

        lax.fori_loop(0, t // PAGE_SIZE, body, 0)
        kvc[...] = (h1[pl.ds(0, ncp), :] + h2[pl.ds(1, ncp), :]).astype(BF16)

    q = q_ref[0]
    kc = kvc[...]
    pos = qi * tq + _iota((tq, ncp), 0)
    n = _iota((tq, ncp), 1)
    mask = (n * CMP_D + CMP_L - 1 <= pos) & (n < nc)
    psum = jnp.zeros((tq, ncp), F32)
    for h in range(nh):
        qh = (_head_block(q, h, NSA_HD) * scale).astype(BF16)
        p = _cmp_softmax(_dot_nt(qh, kc) + bias_ref[h], mask)
        oc_ref[0, :, h * LANE:(h + 1) * LANE] = _dot(p.astype(BF16), kc)
        psum = psum + p
    imp_ref[0] = _dot_sel(psum, ov_ref[...])


def nsa_cmp_prompt(nq, ncmp, wcmp, bias_c, overlap, nc):
    b, t, _ = nq.shape
    nh, _, ncp = bias_c.shape
    nsp = overlap.shape[1]
    hb = t // CMP_D
    return pl.pallas_call(
        functools.partial(_nsa_cmp_prompt_kernel, nh=nh, t=t, nc=nc, ncp=ncp),
        grid=(b, t // TQ),
        in_specs=[pl.BlockSpec((1, TQ, nh * NSA_HD), lambda i, j: (i, j, 0)),
                  pl.BlockSpec((1, t, LANE), lambda i, j: (i, 0, 0)),
                  pl.BlockSpec((CMP_L, LANE), lambda i, j: (0, 0)),
                  pl.BlockSpec((nh, TQ, ncp), lambda i, j: (0, j, 0)),
                  pl.BlockSpec((ncp, nsp), lambda i, j: (0, 0))],
        out_specs=[pl.BlockSpec((1, TQ, nh * LANE), lambda i, j: (i, j, 0)),
                   pl.BlockSpec((1, TQ, nsp), lambda i, j: (i, j, 0))],
        out_shape=[jax.ShapeDtypeStruct((b, t, nh * LANE), F32), jax.ShapeDtypeStruct((b, t, nsp), F32)],
        scratch_shapes=[pltpu.VMEM((ncp, LANE), BF16), pltpu.VMEM((max(hb, ncp) + 8, LANE), F32),
                        pltpu.VMEM((max(hb, ncp) + 8, LANE), F32)],
        compiler_params=_cparams(("arbitrary", "arbitrary")),
        name="nsa_cmp_prompt",
    )(nq, ncmp, wcmp, bias_c, overlap)


def _nsa_cmp_sample_kernel(pt_ref, q_ref, w_ref, bias_ref, ov_ref, pool, oc_ref, imp_ref,
                           buf, sems, kvc, h1, h2, *, layer, nh, ts, npages, nc, ncp, n_steps, past_len):
    g = pl.program_id(0)
    r = ts * nh
    scale = NSA_HD ** -0.5
    per = PAGE_SIZE // CMP_D

    def issue(step, slot, start):
        _pages_dma(pool, layer, pt_ref, step, 0, npages, buf, slot, sems, start)

    slot = _paged_pipeline(g, n_steps, issue)
    h2[...] = jnp.zeros(h2.shape, F32)
    w = w_ref[...]

    def body(c, cy):
        r0 = pl.multiple_of(c * PAGE_SIZE, PAGE_SIZE)
        _pool_page(buf[slot, pl.ds(r0, PAGE_SIZE), :], w, h1, h2, pl.multiple_of(c * per, per))
        return cy

    lax.fori_loop(0, npages, body, 0)
    kvc[...] = (h1[pl.ds(0, ncp), :] + h2[pl.ds(1, ncp), :]).astype(BF16)
    kc = kvc[...]
    qp = _dot((q_ref[0] * scale).astype(BF16), _place(NSA_HD, LANE, 0)).astype(BF16)
    pos = past_len + _iota((r, ncp), 0) // nh
    n = _iota((r, ncp), 1)
    mask = (n * CMP_D + CMP_L - 1 <= pos) & (n < nc)
    p = _cmp_softmax(_dot_nt(qp, kc) + bias_ref[...], mask)
    oc_ref[0] = _dot(p.astype(BF16), kc)
    rows = (_iota((ts, r), 1) // nh == _iota((ts, r), 0)).astype(BF16)
    imp_ref[0] = _dot_sel(_sel_dot(rows, p), ov_ref[...])


def nsa_cmp_sample(layer, page_table, nq, wcmp, bias_c, overlap, pool, ts, nc):
    bs, r, _ = nq.shape
    nh = r // ts
    npages = page_table.shape[1]
    ncp = bias_c.shape[1]
    nsp = overlap.shape[1]
    hb = npages * (PAGE_SIZE // CMP_D)
    assert ncp == hb
    kern = functools.partial(_nsa_cmp_sample_kernel, layer=layer, nh=nh, ts=ts, npages=npages, nc=nc,
                             ncp=ncp, n_steps=bs, past_len=npages * PAGE_SIZE)
    per_b = lambda w: pl.BlockSpec((1,) + w, lambda g, pt: (g, 0, 0))
    const = lambda s: pl.BlockSpec(s, lambda g, pt: (0, 0))
    grid_spec = pltpu.PrefetchScalarGridSpec(
        num_scalar_prefetch=1, grid=(bs,),
        in_specs=[per_b((r, NSA_HD)), const((CMP_L, LANE)), const((r, ncp)), const((ncp, nsp)),
                  pl.BlockSpec(memory_space=pl.ANY)],
        out_specs=[per_b((r, LANE)), per_b((ts, nsp))],
        scratch_shapes=[pltpu.VMEM((2, npages * PAGE_SIZE, LANE), F32), pltpu.SemaphoreType.DMA((2,)),
                        pltpu.VMEM((ncp, LANE), BF16), pltpu.VMEM((hb + 8, LANE), F32),
                        pltpu.VMEM((hb + 8, LANE), F32)])
    return pl.pallas_call(
        kern, grid_spec=grid_spec,
        out_shape=[jax.ShapeDtypeStruct((bs, r, LANE), F32), jax.ShapeDtypeStruct((bs, ts, nsp), F32)],
        compiler_params=_cparams(("arbitrary",)), name="nsa_cmp_sample",
    )(page_table, nq, wcmp, bias_c, overlap, pool)


def _topk_kernel(imp_ref, pos_ref, o_ref, score, *, ns, n_sel):
    imp = imp_ref[...]
    pos = pos_ref[...]
    blk = _iota(imp.shape, 0)
    cur = pos // SLC_B
    forced = (blk == 0) | (blk == cur) | (blk == cur - 1)
    sc = jnp.where(forced, BIG, jnp.where(blk * SLC_B <= pos, imp, -BIG))
    sc = jnp.where(blk < ns, sc, -3e38)
    score[...] = sc

    def body(j, rank):
        sj = score[pl.ds(j, 1), :]
        tie = jnp.where(blk > j, 1.0, 0.0)
        return rank + jnp.where(sj > sc, 1.0, jnp.where(sj == sc, tie, 0.0))

    rank = lax.fori_loop(0, ns, body, jnp.zeros(imp.shape, F32))
    o_ref[...] = jnp.where(rank < n_sel, 1.0, 0.0)


def topk_blocks(imp, pos, ns):
    n, nsp = imp.shape
    tn = LANE if n % LANE == 0 else n
    sel_t = pl.pallas_call(
        functools.partial(_topk_kernel, ns=ns, n_sel=min(N_SEL, ns)),
        grid=(n // tn,),
        in_specs=[pl.BlockSpec((nsp, tn), lambda i: (0, i)), pl.BlockSpec((1, tn), lambda i: (0, i))],
        out_specs=pl.BlockSpec((nsp, tn), lambda i: (0, i)),
        out_shape=jax.ShapeDtypeStruct((nsp, n), F32),
        scratch_shapes=[pltpu.VMEM((nsp, tn), F32)],
        compiler_params=_cparams(("parallel",)),
        name="topk_blocks",
    )(imp.T, pos.reshape(1, n).astype(jnp.int32))
    return sel_t.T


def _nsa_main_prompt_kernel(q_ref, gate_ref, oc_ref, sel_ref, slc_ref, win_ref, bt_ref, o_ref,
                            slcb, winb, qs, selm, m_ref, l_ref, acc_ref, *, nh, t, nsp):
    qi = pl.program_id(1)
    tq = tk = TQ
    scale = NSA_HD ** -0.5

    @pl.when(qi == 0)
    def _cast():
        slcb[...] = slc_ref[0].astype(BF16)
        winb[...] = win_ref[0].astype(BF16)

    q = q_ref[0]
    for h in range(nh):
        qs[h] = (_head_block(q, h, NSA_HD) * scale).astype(BF16)
    expand = (_iota((nsp, t), 1) // SLC_B == _iota((nsp, t), 0)).astype(BF16)
    selm[...] = _dot(sel_ref[0].astype(BF16), expand)
    _flash_init(m_ref, l_ref, acc_ref)
    dq = _iota((tq, tk), 0) - _iota((tq, tk), 1)

    def step(kt, cy):
        r0 = pl.multiple_of(kt * tk, tk)
        dist = (qi - kt) * tk + dq
        b0 = pl.multiple_of(jnp.minimum(qi - kt, 2) * tk, tk)
        kc = slcb[pl.ds(r0, tk), :]
        mask_s = jnp.where(dist >= 0, selm[:, pl.ds(r0, tk)], 0.0) > 0.5
        for h in range(nh):
            s = _dot_nt(qs[h], kc) + bt_ref[h, pl.ds(b0, tk), :]
            _flash_update(s, kc, m_ref, l_ref, acc_ref, h, mask_s)

        @pl.when((qi - kt) * tk < WINDOW + tq)
        def _win():
            kw = winb[pl.ds(r0, tk), :]
            mask_w = jnp.where(dist >= 0, dist, WINDOW) < WINDOW
            for h in range(nh):
                s = _dot_nt(qs[h], kw) + bt_ref[h, pl.ds(b0, tk), :]
                _flash_update(s, kw, m_ref, l_ref, acc_ref, nh + h, mask_w)
        return cy

    lax.fori_loop(0, qi + 1, step, 0)
    g = jax.nn.sigmoid(gate_ref[0])
    for h in range(nh):
        o = (g[:, 3 * h:3 * h + 1] * oc_ref[0, :, h * LANE:(h + 1) * LANE]
             + g[:, 3 * h + 1:3 * h + 2] * _flash_out(l_ref, acc_ref, h)
             + g[:, 3 * h + 2:3 * h + 3] * _flash_out(l_ref, acc_ref, nh + h))
        o_ref[0, :, h * NSA_HD:(h + 1) * NSA_HD] = o[:, NSA_HD:].astype(o_ref.dtype)


def nsa_main_prompt(nq, gates, o_c, sel, nslc, nwin, btiles):
    b, t, _ = nq.shape
    nh = btiles.shape[0]
    nsp = sel.shape[-1]
    qblk = lambda w: pl.BlockSpec((1, TQ, w), lambda i, j: (i, j, 0))
    full = lambda w: pl.BlockSpec((1, t, w), lambda i, j: (i, 0, 0))
    return pl.pallas_call(
        functools.partial(_nsa_main_prompt_kernel, nh=nh, t=t, nsp=nsp),
        grid=(b, t // TQ),
        in_specs=[qblk(nh * NSA_HD), qblk(LANE), qblk(nh * LANE), qblk(nsp), full(LANE), full(LANE),
                  pl.BlockSpec(btiles.shape, lambda i, j: (0, 0, 0))],
        out_specs=qblk(nh * NSA_HD),
        out_shape=jax.ShapeDtypeStruct((b, t, nh * NSA_HD), BF16),
        scratch_shapes=[pltpu.VMEM((t, LANE), BF16), pltpu.VMEM((t, LANE), BF16),
                        pltpu.VMEM((nh, TQ, LANE), BF16), pltpu.VMEM((TQ, t), F32),
                        pltpu.VMEM((2 * nh, TQ, 1), F32), pltpu.VMEM((2 * nh, TQ, 1), F32),
                        pltpu.VMEM((2 * nh, TQ, LANE), F32)],
        compiler_params=_cparams(("arbitrary", "arbitrary")),
        name="nsa_main_prompt",
    )(nq, gates, o_c, sel, nslc, nwin, btiles)


def _nsa_main_sample_kernel(pt_ref, q_ref, gate_ref, oc_ref, sel_ref, slcn_ref, win_ref,
                            blast_ref, bnew_ref, bwin_ref, bfar_ref, pool, o_ref,
                            buf, sems, qp, selr, m_ref, l_ref, acc_ref,
                            *, layer, nh, ts, nchunk, cpages, tk, n_steps, past_len, wb):
    g = pl.program_id(0)
    c = g % nchunk
    r = ts * nh
    scale = NSA_HD ** -0.5
    nsp = sel_ref.shape[-1]

    def issue(step, slot, start):
        _pages_dma(pool, layer, pt_ref, step // nchunk, (step % nchunk) * cpages, cpages, buf, slot, sems, start)

    slot = _paged_pipeline(g, n_steps, issue)

    @pl.when(c == 0)
    def _prep():
        qp[...] = _dot((q_ref[0] * scale).astype(BF16), _place(NSA_HD, LANE, 0)).astype(BF16)
        rows = (_iota((r, ts), 1) == _iota((r, ts), 0) // nh).astype(BF16)
        selr[...] = _dot(rows, sel_ref[0].astype(BF16)).astype(BF16)
        _flash_init(m_ref, l_ref, acc_ref)

    ntile = cpages * PAGE_SIZE // tk

    def tile(i, cy):
        r0 = pl.multiple_of(i * tk, tk)
        base = c * (cpages * PAGE_SIZE) + r0
        expand = ((base + _iota((nsp, tk), 1)) // SLC_B == _iota((nsp, tk), 0)).astype(BF16)
        mask = _dot(selr[...], expand) > 0.5
        kc = buf[slot, pl.ds(r0, tk), :].astype(BF16)
        last = jnp.logical_and(c == nchunk - 1, i == ntile - 1)
        s = _dot_nt(qp[...], kc) + jnp.where(last, blast_ref[...], bfar_ref[...])
        _flash_update(s, kc, m_ref, l_ref, acc_ref, 0, mask)
        return cy

    lax.fori_loop(0, ntile, tile, 0)

    @pl.when(c == nchunk - 1)
    def _fin():
        qi = _iota((r, NEW_PAD), 0) // nh
        kj = _iota((r, NEW_PAD), 1)
        kn = slcn_ref[0].astype(BF16)
        _flash_update(_dot_nt(qp[...], kn) + bnew_ref[...], kn, m_ref, l_ref, acc_ref, 0,
                      (kj <= qi) & (kj < ts))
        nw = wb + NEW_PAD
        kw = win_ref[0].astype(BF16)
        col = _iota((r, nw), 1)
        kpos = jnp.where(col < wb, past_len - wb + col, past_len + col - wb)
        dist = past_len + _iota((r, nw), 0) // nh - kpos
        mask_w = (dist >= 0) & (dist < WINDOW) & (col < wb + ts)
        _flash_update(_dot_nt(qp[...], kw) + bwin_ref[...], kw, m_ref, l_ref, acc_ref, 1, mask_w)
        gt = jax.nn.sigmoid(gate_ref[0])
        o = (gt[:, 0:1] * oc_ref[0] + gt[:, 1:2] * _flash_out(l_ref, acc_ref, 0)
             + gt[:, 2:3] * _flash_out(l_ref, acc_ref, 1))
        o_ref[0] = o.astype(o_ref.dtype)


def nsa_main_sample(layer, page_table, nq, gates, o_c, sel, slc_new, win_kv, b_last, b_new, b_win, b_far,
                    pool, ts):
    bs, r, _ = nq.shape
    nh = r // ts
    npages = page_table.shape[1]
    nchunk = 2 if npages % 2 == 0 and npages >= 16 else 1
    cpages = npages // nchunk
    tk = b_last.shape[1]
    n_steps = bs * nchunk
    wb = win_kv.shape[1] - NEW_PAD
    nsp = sel.shape[-1]
    kern = functools.partial(_nsa_main_sample_kernel, layer=layer, nh=nh, ts=ts, nchunk=nchunk, cpages=cpages,
                             tk=tk, n_steps=n_steps, past_len=npages * PAGE_SIZE, wb=wb)
    per_b = lambda w: pl.BlockSpec((1,) + w, lambda g, pt: (g // nchunk, 0, 0))
    const = lambda a: pl.BlockSpec(a.shape, lambda g, pt: (0, 0))
    rows = cpages * PAGE_SIZE
    scratch = [pltpu.VMEM((2, rows, LANE), F32), pltpu.SemaphoreType.DMA((2,)),
               pltpu.VMEM((r, LANE), BF16), pltpu.VMEM((r, nsp), BF16),
               pltpu.VMEM((2, r, 1), F32), pltpu.VMEM((2, r, 1), F32), pltpu.VMEM((2, r, LANE), F32)]
    return _paged_call(
        kern, n_steps,
        [per_b((r, NSA_HD)), per_b((r, N_BRANCH)), per_b((r, LANE)), per_b((ts, nsp)),
         per_b((NEW_PAD, LANE)), per_b((wb + NEW_PAD, LANE)),
         const(b_last), const(b_new), const(b_win), const(b_far)], 1,
        per_b((r, LANE)), jax.ShapeDtypeStruct((bs, r, LANE), BF16), scratch, "nsa_main_sample",
    )(page_table, nq, gates, o_c, sel, _pad_new(slc_new), win_kv, b_last, b_new, b_win, b_far, pool)


def _conv_prompt_kernel(g_ref, halo_ref, u_ref, w_ref, b_ref, o_ref, ext):
    ti = pl.program_id(1)
    tt = g_ref.shape[1]
    ext[0:8, :] = jnp.where(ti == 0, 0.0, halo_ref[0])
    ext[8:, :] = g_ref[0]
    c = (b_ref[...] + ext[pl.ds(8, tt), :] * w_ref[2:3, :] + ext[pl.ds(7, tt), :] * w_ref[1:2, :]
         + ext[pl.ds(6, tt), :] * w_ref[0:1, :])
    o_ref[0] = (c * jax.nn.sigmoid(c) * u_ref[0]).astype(o_ref.dtype)


def conv_act_prompt(g, u, w_conv, b_conv):
    b, t, f = g.shape
    tt = _row_tile(t, 512)
    tf = max(x for x in range(LANE, min(f, 2048) + 1, LANE) if f % x == 0)
    hb = tt // 8
    blk = pl.BlockSpec((1, tt, tf), lambda i, j, k: (i, j, k))
    vec = lambda n: pl.BlockSpec((n, tf), lambda i, j, k: (0, k))
    return pl.pallas_call(
        _conv_prompt_kernel,
        grid=(b, t // tt, f // tf),
        in_specs=[blk, pl.BlockSpec((1, 8, tf), lambda i, j, k: (i, jnp.maximum(j * hb - 1, 0), k)),
                  blk, vec(CONV_W), vec(1)],
        out_specs=blk,
        out_shape=jax.ShapeDtypeStruct((b, t, f), BF16),
        scratch_shapes=[pltpu.VMEM((tt + 8, tf), F32)],
        compiler_params=_cparams(("parallel", "parallel", "parallel")),
        name="conv_act_prompt",
    )(g, g, u, w_conv, b_conv.reshape(1, f))


def _conv_rows_kernel(g0_ref, g1_ref, g2_ref, u_ref, w_ref, b_ref, o_ref):
    c = b_ref[...] + g0_ref[...] * w_ref[0:1, :] + g1_ref[...] * w_ref[1:2, :] + g2_ref[...] * w_ref[2:3, :]
    o_ref[...] = (c * jax.nn.sigmoid(c) * u_ref[...]).astype(o_ref.dtype)


def conv_act_rows(g0, g1, g2, u, w_conv, b_conv):
    n, f = u.shape
    tf = max(x for x in range(LANE, min(f, 2048) + 1, LANE) if f % x == 0)
    blk = pl.BlockSpec((n, tf), lambda k: (0, k))
    vec = lambda m: pl.BlockSpec((m, tf), lambda k: (0, k))
    return pl.pallas_call(
        _conv_rows_kernel,
        grid=(f // tf,),
        in_specs=[blk, blk, blk, blk, vec(CONV_W), vec(1)],
        out_specs=blk,
        out_shape=jax.ShapeDtypeStruct((n, f), BF16),
        compiler_params=_cparams(("parallel",)),
        name="conv_act_rows",
    )(g0, g1, g2, u, w_conv, b_conv.reshape(1, f))


def kernel(x_prompt, x_sample, cache_fox_kv, cache_fox_logf, cache_mla_ckv, cache_mla_krope, cache_nsa_cmp_kv, cache_nsa_slc_kv, state_nsa_win_kv, state_conv, page_table, g_attn, w_in, b_fgate, g_qa, wq_b, g_kva, wkv_b, w_cmp, rel_bias_table, w_out, g_ffn, w_gate, w_up, w_conv, b_conv, w_down, g_final):
    b, t, d = x_prompt.shape
    bs, ts, _ = x_sample.shape
    depth = w_in.shape[0]
    nh = d // 256
    q_lora, kv_lora = g_qa.shape[1], g_kva.shape[1]
    past_len = page_table.shape[1] * PAGE_SIZE
    n_p, n_s = b * t, bs * ts
    lay = _in_layout(nh, q_lora, kv_lora)
    pos_p = jnp.arange(t, dtype=jnp.int32)
    pos_s = past_len + jnp.arange(ts, dtype=jnp.int32)
    pos_all = jnp.concatenate([jnp.tile(pos_p, b), jnp.tile(pos_s, bs)])
    cos_k, sin_k = _rope_tables(pos_all, LANE // QK_ROPE)
    cos_q, sin_q = _rope_tables(pos_all, nh)
    x = jnp.concatenate([x_prompt.reshape(n_p, d), x_sample.reshape(n_s, d)], axis=0)
    w_buf = state_nsa_win_kv.shape[2]
    f_ff = w_gate.shape[2]
    assert t % TQ == 0 and ts < CMP_D and ts <= NEW_PAD

    ratio = CMP_L // CMP_D
    nc_p = t // CMP_D - ratio + 1
    ncp_p = _pad_to(nc_p, LANE)
    ns_p = -(-t // SLC_B)
    nsp_p = _pad_to(ns_p, LANE)
    t_s = past_len + ts
    nc_s = t_s // CMP_D - ratio + 1
    ncp_s = (past_len // CMP_D)
    ns_s = -(-t_s // SLC_B)
    nsp_s = _pad_to(ns_s, LANE)
    tk_s = min(256, past_len)
    ar = np.arange
    cmp_end = lambda n: ar(n) * CMP_D + CMP_L - 1
    d_tile = ar(TQ)[:, None] - ar(TQ)[None, :]
    bk_tiles = np.concatenate([_bucket_np(d_tile), _bucket_np(TQ + d_tile),
                               np.full((TQ, TQ), NUM_BUCKETS - 1, np.int32)], axis=0)
    bk_cmp_p = _bucket_np(ar(t)[:, None] - cmp_end(ncp_p)[None, :])
    ps = past_len + ar(ts)[:, None]
    col_w = ar(w_buf + NEW_PAD)[None, :]
    kpos_w = np.where(col_w < w_buf, past_len - w_buf + col_w, past_len + col_w - w_buf)
    bk_s = np.concatenate([_bucket_np(ps - cmp_end(ncp_s)[None, :]),
                           _bucket_np(ps - (past_len - tk_s + ar(tk_s))[None, :]),
                           _bucket_np(ar(ts)[:, None] - ar(NEW_PAD)[None, :]),
                           _bucket_np(ps - kpos_w)], axis=1)
    btiles = bias_lookup(rel_bias_table, bk_tiles)
    bias_c_p = bias_lookup(rel_bias_table, bk_cmp_p)
    bias_s = jnp.transpose(bias_lookup(rel_bias_table, bk_s), (1, 0, 2)).reshape(ts * nh, -1)
    cuts = np.cumsum([ncp_s, tk_s, NEW_PAD])
    bias_c_s, b_last, b_new, b_win = (bias_s[:, :cuts[0]], bias_s[:, cuts[0]:cuts[1]],
                                      bias_s[:, cuts[1]:cuts[2]], bias_s[:, cuts[2]:])
    b_far = jnp.tile(rel_bias_table[NUM_BUCKETS - 1].astype(F32), ts).reshape(ts * nh, 1)
    overlap_p = jnp.asarray(_overlap_np(ncp_p, nsp_p, nc_p, ns_p), BF16)
    overlap_s = jnp.asarray(_overlap_np(ncp_s, nsp_s, nc_s, ns_s), BF16)
    posr_p = jnp.tile(pos_p, b)
    posr_s = jnp.tile(pos_s, bs)

    pool_shape = cache_fox_kv.shape[:3]
    fox_pool = cache_fox_kv.reshape(pool_shape + (2 * FOX_HD,))
    cmp_pool = cache_nsa_cmp_kv.reshape(pool_shape + (2 * NSA_HD,))
    slc_pool = cache_nsa_slc_kv.reshape(pool_shape + (2 * NSA_HD,))

    states = []
    for l in range(depth):
        hn = rmsnorm_rows(x, g_attn[l], BF16)
        p = matmul(hn, _pad_w_in(w_in[l], lay))
        logf, cqn, ckv, krope = post_proj(p, b_fgate[l], g_qa[l], g_kva[l], cos_k, sin_k, lay)
        wq = jnp.concatenate([wq_b[l][:, :, :QK_NOPE].reshape(q_lora, -1),
                              wq_b[l][:, :, QK_NOPE:].reshape(q_lora, -1)], axis=1).astype(BF16)
        q_mla = matmul(cqn, wq)
        q_rope = q_rope_rows(q_mla, nh * QK_NOPE, cos_q, sin_q)
        w_uk = jnp.transpose(wkv_b[l][:, :, :QK_NOPE], (1, 2, 0)).astype(BF16)
        w_uv = jnp.transpose(wkv_b[l][:, :, QK_NOPE:], (1, 0, 2)).astype(BF16)
        q_lat = head_matmul(q_mla, w_uk, F32)
        seg = lambda name, width=None: p[:, lay[name][0]:lay[name][0] + (width or lay[name][1])]
        fq, fkv, nq = seg("fq"), seg("fkv"), seg("nq")
        ncmp, nslc, nwin = seg("ncmp"), seg("nslc"), seg("nwin")
        wcmp = jnp.concatenate([w_cmp[l][0], w_cmp[l][1]], axis=1).astype(F32)
        pr = lambda a: a[:n_p].reshape(b, t, -1)
        sm = lambda a: a[n_p:].reshape(bs, ts, -1)
        smh = lambda a, w: a[n_p:].reshape(bs, ts * nh, w)

        o_fox_p = fox_prompt(pr(fq), pr(fkv), pr(logf))
        o_lat_p = mla_prompt(pr(q_lat), pr(q_rope), pr(ckv), pr(krope))
        oc_p, imp_p = nsa_cmp_prompt(pr(nq), pr(ncmp), wcmp, bias_c_p, overlap_p, nc_p)
        sel_p = topk_blocks(imp_p.reshape(n_p, nsp_p), posr_p, ns_p).reshape(b, t, nsp_p)
        o_nsa_p = nsa_main_prompt(pr(nq), pr(seg("ngate", LANE)), oc_p, sel_p, pr(nslc), pr(nwin), btiles)

        o_fox_s = fox_sample(l, page_table, smh(fq, FOX_HD), sm(fkv), sm(logf), fox_pool, cache_fox_logf)
        o_lat_s = mla_sample(l, page_table, smh(q_lat, kv_lora), smh(q_rope, QK_ROPE), sm(ckv), sm(krope),
                             cache_mla_ckv, cache_mla_krope)
        oc_s, imp_s = nsa_cmp_sample(l, page_table, smh(nq, NSA_HD), wcmp, bias_c_s, overlap_s, cmp_pool,
                                     ts, nc_s)
        sel_s = topk_blocks(imp_s.reshape(n_s, nsp_s), posr_s, ns_s).reshape(bs, ts, nsp_s)
        win_all = jnp.concatenate([state_nsa_win_kv[l].reshape(bs, w_buf, -1), sm(nwin)], axis=1)
        win_kv = jnp.pad(win_all, ((0, 0), (0, NEW_PAD - ts), (0, 0)))
        o_nsa_s = nsa_main_sample(l, page_table, smh(nq, NSA_HD), smh(seg("ngate"), N_BRANCH), oc_s, sel_s,
                                  sm(nslc), win_kv, b_last, b_new, b_win, b_far, slc_pool, ts)

        o_fox = jnp.concatenate([o_fox_p.reshape(n_p, -1), o_fox_s[..., FOX_HD:].reshape(n_s, -1)], axis=0)
        o_lat = jnp.concatenate([o_lat_p.reshape(n_p, -1), o_lat_s.reshape(n_s, -1)], axis=0)
        o_nsa = jnp.concatenate([o_nsa_p.reshape(n_p, -1), o_nsa_s[..., NSA_HD:].reshape(n_s, -1)], axis=0)
        o_mla = head_matmul(o_lat, w_uv, BF16)
        mix = jnp.concatenate([o_fox, o_mla, o_nsa], axis=1)
        x = matmul(mix, w_out[l].astype(BF16), res=x)

        h2 = rmsnorm_rows(x, g_ffn[l], BF16)
        gg = matmul(h2, w_gate[l].astype(BF16))
        uu = matmul(h2, w_up[l].astype(BF16))
        gg_p, gg_s = gg[:n_p].reshape(b, t, f_ff), gg[n_p:].reshape(bs, ts, f_ff)
        act_p = conv_act_prompt(gg_p, uu[:n_p].reshape(b, t, f_ff), w_conv[l], b_conv[l])
        ext_s = jnp.concatenate([state_conv[l], gg_s], axis=1)
        act_s = conv_act_rows(*(ext_s[:, k:k + ts].reshape(n_s, f_ff) for k in range(CONV_W)),
                              uu[n_p:], w_conv[l], b_conv[l])
        act = jnp.concatenate([act_p.reshape(n_p, f_ff), act_s], axis=0)
        x = matmul(act, w_down[l].astype(BF16), res=x)

        ext_p = jnp.concatenate([jnp.zeros((b, CONV_W - 1, f_ff), F32), gg_p], axis=1)
        win_p = jnp.pad(pr(nwin), ((0, 0), (w_buf, 0), (0, 0)))[:, -w_buf:]
        kv4 = lambda a, n: a.reshape(a.shape[0], n, 2, -1)
        states.append((
            kv4(pr(fkv), t), kv4(sm(fkv), ts), pr(logf), sm(logf), pr(ckv), sm(ckv), pr(krope), sm(krope),
            kv4(pr(ncmp), t), kv4(sm(ncmp), ts), kv4(pr(nslc), t), kv4(sm(nslc), ts),
            kv4(win_p, w_buf), kv4(win_all[:, -w_buf:], w_buf),
            ext_p[:, -(CONV_W - 1):], ext_s[:, -(CONV_W - 1):]))

    y = rmsnorm_rows(x, g_final, F32)
    stacked = [jnp.stack(z) for z in zip(*states)]
    return (y[:n_p].reshape(b, t, d), y[n_p:].reshape(bs, ts, d)) + tuple(stacked)
```

```python
import functools
import math

import numpy as np
import jax
import jax.numpy as jnp
from jax import lax
from jax.experimental import pallas as pl
from jax.experimental.pallas import tpu as pltpu

F32 = jnp.float32
BF16 = jnp.bfloat16

PAGE_SIZE = 128
FOX_HD = 64
MLA_VHD = 128
QK_NOPE = 128
QK_ROPE = 32
ROPE_THETA = 10000.0
NSA_HD = 64
CMP_L = 32
CMP_D = 16
SLC_B = 64
N_SEL = 16
WINDOW = 512
N_BRANCH = 3
NUM_BUCKETS = 32
MAX_DIST = 128
CONV_W = 3
EPS = 1e-6
NEG = -1e30
BIG = 1e30

LANE = 128
VMEM_LIMIT = 56 * 1024 * 1024
TQ = 128


def _cparams(sem):
    return pltpu.CompilerParams(dimension_semantics=sem, vmem_limit_bytes=VMEM_LIMIT)


def _row_tile(n, cap):
    best = None
    for t in range(8, min(n, cap) + 1, 8):
        if n % t == 0:
            best = t
    return n if best is None else best


def _split3(x):
    a = x.astype(BF16)
    r = x - a.astype(F32)
    b = r.astype(BF16)
    c = (r - b.astype(F32)).astype(BF16)
    return a, b, c


def _dot(a, b):
    return jnp.dot(a, b, preferred_element_type=F32)


def _dot_nt(a, b):
    return lax.dot_general(a, b, (((1,), (1,)), ((), ())), preferred_element_type=F32)


def _sel_dot(sel, x):
    s = sel.astype(BF16)
    a, b, c = _split3(x)
    return _dot(s, a) + _dot(s, b) + _dot(s, c)


def _dot_sel(x, sel):
    s = sel.astype(BF16)
    a, b, c = _split3(x)
    return _dot(a, s) + _dot(b, s) + _dot(c, s)


def _iota(shape, dim):
    return lax.broadcasted_iota(jnp.int32, shape, dim)


def _rms_kernel(x_ref, g_ref, o_ref):
    x = x_ref[...].astype(F32)
    y = x * lax.rsqrt(jnp.mean(x * x, axis=-1, keepdims=True) + EPS)
    o_ref[...] = (y * g_ref[...].astype(F32)).astype(o_ref.dtype)


def rmsnorm_rows(x, g, out_dtype):
    n, d = x.shape
    tm = _row_tile(n, 512)
    return pl.pallas_call(
        _rms_kernel,
        grid=(n // tm,),
        in_specs=[pl.BlockSpec((tm, d), lambda i: (i, 0)), pl.BlockSpec((1, d), lambda i: (0, 0))],
        out_specs=pl.BlockSpec((tm, d), lambda i: (i, 0)),
        out_shape=jax.ShapeDtypeStruct((n, d), out_dtype),
        compiler_params=_cparams(("parallel",)),
        name="rmsnorm",
    )(x, g.reshape(1, d))


def _mm_kernel(*refs, nk, has_res):
    if has_res:
        a_ref, w_ref, r_ref, o_ref = refs[:4]
        scratch = refs[4:]
    else:
        a_ref, w_ref, o_ref = refs[:3]
        r_ref = None
        scratch = refs[3:]
    part = _dot(a_ref[...].astype(BF16), w_ref[...])
    if nk == 1:
        if has_res:
            part = part + r_ref[...]
        o_ref[...] = part.astype(o_ref.dtype)
        return
    acc_ref, = scratch
    k = pl.program_id(2)

    @pl.when(k == 0)
    def _():
        acc_ref[...] = part

    @pl.when(k > 0)
    def _():
        acc_ref[...] += part

    @pl.when(k == nk - 1)
    def _():
        out = acc_ref[...]
        if has_res:
            out = out + r_ref[...]
        o_ref[...] = out.astype(o_ref.dtype)


def matmul(a, w, res=None, out_dtype=F32, tm_cap=1088, tn_cap=512, tk_cap=4096):
    n, kdim = a.shape
    m = w.shape[1]
    tm = _row_tile(n, tm_cap)
    tn = max(t for t in range(LANE, min(m, tn_cap) + 1, LANE) if m % t == 0)
    tk = kdim if kdim <= tk_cap else max(t for t in range(LANE, tk_cap + 1, LANE) if kdim % t == 0)
    nk = kdim // tk
    in_specs = [pl.BlockSpec((tm, tk), lambda i, j, k: (i, k)),
                pl.BlockSpec((tk, tn), lambda i, j, k: (k, j))]
    args = [a, w]
    if res is not None:
        in_specs.append(pl.BlockSpec((tm, tn), lambda i, j, k: (i, j)))
        args.append(res)
    return pl.pallas_call(
        functools.partial(_mm_kernel, nk=nk, has_res=res is not None),
        grid=(n // tm, m // tn, nk),
        in_specs=in_specs,
        out_specs=pl.BlockSpec((tm, tn), lambda i, j, k: (i, j)),
        out_shape=jax.ShapeDtypeStruct((n, m), out_dtype),
        scratch_shapes=[pltpu.VMEM((tm, tn), F32)] if nk > 1 else [],
        compiler_params=_cparams(("parallel", "parallel", "arbitrary")),
        name="matmul",
    )(*args)


def _headmm_kernel(a_ref, w_ref, o_ref):
    o_ref[...] = _dot(a_ref[...].astype(BF16), w_ref[0]).astype(o_ref.dtype)


def head_matmul(a, w, out_dtype):
    n = a.shape[0]
    h, ka, kb = w.shape
    tm = _row_tile(n, 1088)
    return pl.pallas_call(
        _headmm_kernel,
        grid=(n // tm, h),
        in_specs=[pl.BlockSpec((tm, ka), lambda i, j: (i, j)),
                  pl.BlockSpec((1, ka, kb), lambda i, j: (j, 0, 0))],
        out_specs=pl.BlockSpec((tm, kb), lambda i, j: (i, j)),
        out_shape=jax.ShapeDtypeStruct((n, h * kb), out_dtype),
        compiler_params=_cparams(("parallel", "parallel")),
        name="head_matmul",
    )(a, w)


def _pad_to(n, m):
    return -(-n // m) * m


def _in_layout(h, q_lora, kv_lora):
    sizes = [("fq", h * FOX_HD), ("fkv", 2 * FOX_HD), ("ff", h), ("cq", q_lora), ("ckv", kv_lora),
             ("kr", QK_ROPE), ("nq", h * NSA_HD), ("ncmp", 2 * NSA_HD), ("nslc", 2 * NSA_HD),
             ("nwin", 2 * NSA_HD), ("ngate", N_BRANCH * h)]
    off, lay = 0, {}
    for name, sz in sizes:
        lay[name] = (off, sz)
        off += _pad_to(sz, LANE)
    lay["total"] = _pad_to(off, 512)
    return lay


def _pad_w_in(w, lay):
    d = w.shape[0]
    order = ["fq", "fkv", "ff", "cq", "ckv", "kr", "nq", "ncmp", "nslc", "nwin", "ngate"]
    cols, src, pos = [], 0, 0
    for name in order:
        off, sz = lay[name]
        if off > pos:
            cols.append(jnp.zeros((d, off - pos), w.dtype))
        cols.append(w[:, src:src + sz])
        src += sz
        pos = off + sz
    assert src == w.shape[1], (src, w.shape)
    if lay["total"] > pos:
        cols.append(jnp.zeros((d, lay["total"] - pos), w.dtype))
    return jnp.concatenate(cols, axis=1).astype(BF16)


def _rope_lanes(x, cosf, sins, half):
    n = x.shape[-1]
    lane = _iota(x.shape, x.ndim - 1)
    fwd = pltpu.roll(x, n - half, x.ndim - 1)
    bwd = pltpu.roll(x, half, x.ndim - 1)
    partner = jnp.where(lane % (2 * half) < half, fwd, bwd)
    return x * cosf + partner * sins


def _post_kernel(p_ref, bf_ref, gq_ref, gkv_ref, cos_ref, sin_ref,
                 logf_ref, cqn_ref, ckv_ref, kr_ref, *, lay):
    o, s = lay["ff"]
    ff = p_ref[:, o:o + LANE]
    z = ff + bf_ref[...]
    logf = jnp.minimum(z, 0.0) - jnp.log1p(jnp.exp(-jnp.abs(z)))
    logf_ref[...] = logf[:, :s]
    o, s = lay["cq"]
    x = p_ref[:, o:o + s]
    y = x * lax.rsqrt(jnp.mean(x * x, axis=-1, keepdims=True) + EPS)
    cqn_ref[...] = (y * gq_ref[...]).astype(cqn_ref.dtype)
    o, s = lay["ckv"]
    x = p_ref[:, o:o + s]
    y = x * lax.rsqrt(jnp.mean(x * x, axis=-1, keepdims=True) + EPS)
    ckv_ref[...] = y * gkv_ref[...]
    o, s = lay["kr"]
    x = p_ref[:, o:o + LANE]
    kr_ref[...] = _rope_lanes(x, cos_ref[...], sin_ref[...], QK_ROPE // 2)[:, :s]


def post_proj(p, b_fgate, g_qa, g_kva, cos_k, sin_k, lay):
    n, wtot = p.shape
    h = lay["ff"][1]
    q_lora, kv_lora = lay["cq"][1], lay["ckv"][1]
    tm = _row_tile(n, 512)
    bf = jnp.zeros((1, LANE), F32).at[0, :h].set(b_fgate.astype(F32))
    row = lambda w: pl.BlockSpec((tm, w), lambda i: (i, 0))
    full = lambda w: pl.BlockSpec((1, w), lambda i: (0, 0))
    return pl.pallas_call(
        functools.partial(_post_kernel, lay=lay),
        grid=(n // tm,),
        in_specs=[row(wtot), full(LANE), full(q_lora), full(kv_lora), row(LANE), row(LANE)],
        out_specs=[row(h), row(q_lora), row(kv_lora), row(QK_ROPE)],
        out_shape=[jax.ShapeDtypeStruct((n, h), F32), jax.ShapeDtypeStruct((n, q_lora), BF16),
                   jax.ShapeDtypeStruct((n, kv_lora), F32), jax.ShapeDtypeStruct((n, QK_ROPE), F32)],
        compiler_params=_cparams(("parallel",)),
        name="post_proj",
    )(p, bf, g_qa.reshape(1, -1).astype(F32), g_kva.reshape(1, -1).astype(F32), cos_k, sin_k)


def _qrope_kernel(x_ref, cos_ref, sin_ref, o_ref):
    o_ref[...] = _rope_lanes(x_ref[...], cos_ref[...], sin_ref[...], QK_ROPE // 2)


def q_rope_rows(q_mla, nope_w, cos_q, sin_q):
    n = q_mla.shape[0]
    w = q_mla.shape[1] - nope_w
    assert nope_w % w == 0
    tm = _row_tile(n, 1088)
    return pl.pallas_call(
        _qrope_kernel,
        grid=(n // tm,),
        in_specs=[pl.BlockSpec((tm, w), lambda i: (i, nope_w // w)),
                  pl.BlockSpec((tm, w), lambda i: (i, 0)), pl.BlockSpec((tm, w), lambda i: (i, 0))],
        out_specs=pl.BlockSpec((tm, w), lambda i: (i, 0)),
        out_shape=jax.ShapeDtypeStruct((n, w), F32),
        compiler_params=_cparams(("parallel",)),
        name="q_rope",
    )(q_mla, cos_q, sin_q)


def _rope_tables(pos, groups):
    half = QK_ROPE // 2
    freq = ROPE_THETA ** (-jnp.arange(half, dtype=F32) / half)
    ang = pos.astype(F32)[:, None] * freq
    cos, sin = jnp.cos(ang), jnp.sin(ang)
    cosf = jnp.tile(jnp.concatenate([cos, cos], axis=1), (1, groups))
    sins = jnp.tile(jnp.concatenate([-sin, sin], axis=1), (1, groups))
    return cosf, sins


def _flash_init(m_ref, l_ref, acc_ref):
    m_ref[...] = jnp.full(m_ref.shape, NEG, F32)
    l_ref[...] = jnp.zeros(l_ref.shape, F32)
    acc_ref[...] = jnp.zeros(acc_ref.shape, F32)


def _flash_update(s, v, m_ref, l_ref, acc_ref, h, mask=None, v_transposed=False):
    if mask is not None:
        s = jnp.where(mask, s, NEG)
    m_old = m_ref[h]
    m_new = jnp.maximum(m_old, jnp.max(s, axis=-1, keepdims=True))
    alpha = jnp.exp(m_old - m_new)
    p = jnp.exp(s - m_new)
    l_ref[h] = alpha * l_ref[h] + jnp.sum(p, axis=-1, keepdims=True)
    p = p.astype(BF16)
    pv = _dot_nt(p, v) if v_transposed else _dot(p, v)
    acc_ref[h] = alpha * acc_ref[h] + pv
    m_ref[h] = m_new


def _flash_out(l_ref, acc_ref, h):
    l = l_ref[h]
    return acc_ref[h] * (1.0 / jnp.where(l == 0.0, 1.0, l))


def _flash_merge(m_ref, l_ref, acc_ref, dst, src):
    m = jnp.maximum(m_ref[dst], m_ref[src])
    a, b = jnp.exp(m_ref[dst] - m), jnp.exp(m_ref[src] - m)
    l_ref[dst] = a * l_ref[dst] + b * l_ref[src]
    acc_ref[dst] = a * acc_ref[dst] + b * acc_ref[src]
    m_ref[dst] = m


def _place(rows, cols, offset):
    return (_iota((rows, cols), 1) == _iota((rows, cols), 0) + offset).astype(BF16)


def _head_block(q, h, hd):
    per = LANE // hd
    blk = q[:, (h // per) * LANE:(h // per + 1) * LANE]
    sh = (h % per) * hd
    if sh:
        blk = pltpu.roll(blk, LANE - sh, 1)
    return jnp.where(_iota(blk.shape, 1) < hd, blk, 0.0)


def _fox_prompt_kernel(q_ref, kv_ref, logf_ref, o_ref, kcat, cum, qs, m_ref, l_ref, acc_ref,
                       *, nh, t, tk):
    qi = pl.program_id(1)
    tq = TQ
    scale = FOX_HD ** -0.5

    @pl.when(qi == 0)
    def _build():
        kcat[:, :LANE] = kv_ref[0].astype(BF16)
        low = (_iota((tk, tk), 1) <= _iota((tk, tk), 0)).astype(BF16)
        places = [_place(nh, LANE, j * nh) for j in range(3)]

        def body(i, carry):
            r0 = pl.multiple_of(i * tk, tk)
            c = _sel_dot(low, logf_ref[0, pl.ds(r0, tk), :]) + carry
            cum[pl.ds(r0, tk), :] = c
            parts = _split3(-c)
            kcat[pl.ds(r0, tk), LANE:] = sum(_dot(p, pm) for p, pm in zip(parts, places)).astype(BF16)
            return c[tk - 1:tk, :]

        lax.fori_loop(0, t // tk, body, jnp.zeros((1, nh), F32))

    q = q_ref[0]
    lane = _iota((tq, LANE), 1)
    for h in range(nh):
        qh = _head_block(q, h, FOX_HD) * scale
        eh = jnp.where((lane < 3 * nh) & (lane % nh == h), 1.0, 0.0)
        qs[h] = jnp.concatenate([qh, eh], axis=1).astype(BF16)
    _flash_init(m_ref, l_ref, acc_ref)
    cq = cum[pl.ds(pl.multiple_of(qi * tq, tq), tq), :]
    qpos = qi * tq + _iota((tq, tk), 0)

    def step(kt, masked):
        r0 = pl.multiple_of(kt * tk, tk)
        kc = kcat[pl.ds(r0, tk), :]
        v = kc[:, :LANE]
        mask = (r0 + _iota((tq, tk), 1) <= qpos) if masked else None
        for h in range(nh):
            s = _dot_nt(qs[h], kc) + cq[:, h:h + 1]
            _flash_update(s, v, m_ref, l_ref, acc_ref, h, mask)

    n_kt = ((qi + 1) * tq + tk - 1) // tk

    def loop_body(kt, carry):
        step(kt, False)
        return carry

    lax.fori_loop(0, n_kt - 1, loop_body, 0)
    step(n_kt - 1, True)
    for h in range(nh):
        o_ref[0, :, h * FOX_HD:(h + 1) * FOX_HD] = _flash_out(l_ref, acc_ref, h)[:, FOX_HD:].astype(o_ref.dtype)


def fox_prompt(fq, fkv, logf):
    b, t, _ = fq.shape
    nh = logf.shape[-1]
    tk = min(256, t)
    return pl.pallas_call(
        functools.partial(_fox_prompt_kernel, nh=nh, t=t, tk=tk),
        grid=(b, t // TQ),
        in_specs=[pl.BlockSpec((1, TQ, nh * FOX_HD), lambda i, j: (i, j, 0)),
                  pl.BlockSpec((1, t, 2 * FOX_HD), lambda i, j: (i, 0, 0)),
                  pl.BlockSpec((1, t, nh), lambda i, j: (i, 0, 0))],
        out_specs=pl.BlockSpec((1, TQ, nh * FOX_HD), lambda i, j: (i, j, 0)),
        out_shape=jax.ShapeDtypeStruct((b, t, nh * FOX_HD), BF16),
        scratch_shapes=[pltpu.VMEM((t, 2 * LANE), BF16), pltpu.VMEM((t, nh), F32),
                        pltpu.VMEM((nh, TQ, 2 * LANE), BF16), pltpu.VMEM((nh, TQ, 1), F32),
                        pltpu.VMEM((nh, TQ, 1), F32), pltpu.VMEM((nh, TQ, LANE), F32)],
        compiler_params=_cparams(("arbitrary", "arbitrary")),
        name="fox_prompt",
    )(fq, fkv, logf)


def _mla_prompt_kernel(ql_ref, qr_ref, ckv_ref, kr_ref, o_ref, kcat, qs, m_ref, l_ref, acc_ref,
                       *, nh, c, tk):
    qi = pl.program_id(1)
    tq = TQ
    scale = (QK_NOPE + QK_ROPE) ** -0.5

    @pl.when(qi == 0)
    def _build():
        kcat[:, :c] = ckv_ref[0].astype(BF16)
        kcat[:, c:] = _dot(kr_ref[0].astype(BF16), _place(QK_ROPE, LANE, 0)).astype(BF16)

    qr = (qr_ref[0] * scale).astype(BF16)
    for h in range(nh):
        sel = (_iota((nh * QK_ROPE, LANE), 0) == _iota((nh * QK_ROPE, LANE), 1) + h * QK_ROPE).astype(BF16)
        qs[h, :, :c] = (ql_ref[0, :, h * c:(h + 1) * c] * scale).astype(BF16)
        qs[h, :, c:] = _dot(qr, sel).astype(BF16)
    _flash_init(m_ref, l_ref, acc_ref)
    qpos = qi * tq + _iota((tq, tk), 0)

    def step(kt, masked):
        r0 = pl.multiple_of(kt * tk, tk)
        kc = kcat[pl.ds(r0, tk), :]
        v = kc[:, :c]
        mask = (r0 + _iota((tq, tk), 1) <= qpos) if masked else None
        for h in range(nh):
            _flash_update(_dot_nt(qs[h], kc), v, m_ref, l_ref, acc_ref, h, mask)

    n_kt = ((qi + 1) * tq + tk - 1) // tk

    def loop_body(kt, carry):
        step(kt, False)
        return carry

    lax.fori_loop(0, n_kt - 1, loop_body, 0)
    step(n_kt - 1, True)
    for h in range(nh):
        o_ref[0, :, h * c:(h + 1) * c] = _flash_out(l_ref, acc_ref, h).astype(o_ref.dtype)


def mla_prompt(q_lat, q_rope, ckv, krope):
    b, t, c = ckv.shape
    nh = q_lat.shape[-1] // c
    tk = min(256, t)
    return pl.pallas_call(
        functools.partial(_mla_prompt_kernel, nh=nh, c=c, tk=tk),
        grid=(b, t // TQ),
        in_specs=[pl.BlockSpec((1, TQ, nh * c), lambda i, j: (i, j, 0)),
                  pl.BlockSpec((1, TQ, nh * QK_ROPE), lambda i, j: (i, j, 0)),
                  pl.BlockSpec((1, t, c), lambda i, j: (i, 0, 0)),
                  pl.BlockSpec((1, t, QK_ROPE), lambda i, j: (i, 0, 0))],
        out_specs=pl.BlockSpec((1, TQ, nh * c), lambda i, j: (i, j, 0)),
        out_shape=jax.ShapeDtypeStruct((b, t, nh * c), BF16),
        scratch_shapes=[pltpu.VMEM((t, c + LANE), BF16), pltpu.VMEM((nh, TQ, c + LANE), BF16),
                        pltpu.VMEM((nh, TQ, 1), F32), pltpu.VMEM((nh, TQ, 1), F32),
                        pltpu.VMEM((nh, TQ, c), F32)],
        compiler_params=_cparams(("arbitrary", "arbitrary")),
        name="mla_prompt",
    )(q_lat, q_rope, ckv, krope)


NEW_PAD = 16


def _pages_dma(pool_ref, layer, pt_ref, b, page0, npages, buf, slot, sem, start, prow=PAGE_SIZE):
    def body(j, c):
        pg = pt_ref[b, page0 + j]
        cp = pltpu.make_async_copy(
            pool_ref.at[layer, pg],
            buf.at[slot, pl.ds(pl.multiple_of(j * prow, prow), prow)],
            sem.at[slot])
        if start:
            cp.start()
        else:
            cp.wait()
        return c
    lax.fori_loop(0, npages, body, 0)


def _paged_pipeline(g, n_steps, issue):
    slot = g % 2

    @pl.when(g == 0)
    def _():
        issue(g, slot, True)

    @pl.when(g + 1 < n_steps)
    def _():
        issue(g + 1, 1 - slot, True)

    issue(g, slot, False)
    return slot


def _pad_rows(x, rows):
    if x.shape[0] == rows:
        return x
    return jnp.concatenate([x, jnp.zeros((rows - x.shape[0],) + x.shape[1:], x.dtype)], axis=0)


def _lane_cat(ref, slot, starts, rows):
    return jnp.concatenate([ref[slot, pl.ds(s, rows), :] for s in starts], axis=1)


def _fox_sample_kernel(pt_ref, q_ref, kvn_ref, lfn_ref, kvpool, lfpool, o_ref,
                       kvbuf, lfbuf, sems, qaug, cnq, knew, suf_s, carry_s, m_ref, l_ref, acc_ref,
                       *, layer, nh, ts, nchunk, cpages, ptile, npar, n_steps):
    g = pl.program_id(0)
    c = g % nchunk
    r = ts * nh
    hp = _pad_to(nh, 16)
    kq = FOX_HD + 3 * hp
    scale = FOX_HD ** -0.5

    def issue(step, slot, start):
        sb = step // nchunk
        p0 = (nchunk - 1 - step % nchunk) * cpages
        _pages_dma(kvpool, layer, pt_ref, sb, p0, cpages, kvbuf, slot, sems.at[0], start)
        _pages_dma(lfpool, layer, pt_ref, sb, p0, cpages, lfbuf, slot, sems.at[1], start, nh)

    slot = _paged_pipeline(g, n_steps, issue)

    @pl.when(c == 0)
    def _prep():
        row = _iota((r, kq), 0)
        lane = _iota((r, kq), 1)
        qp = _dot((q_ref[0] * scale).astype(BF16), _place(FOX_HD, kq, 0))
        eh = jnp.where((lane >= FOX_HD) & ((lane - FOX_HD) % hp == row % nh), 1.0, 0.0)
        qaug[...] = (qp + eh).astype(BF16)
        low = (_iota((NEW_PAD, NEW_PAD), 1) <= _iota((NEW_PAD, NEW_PAD), 0)).astype(BF16)
        cn = _sel_dot(low, lfn_ref[0])
        pick = (_iota((r, NEW_PAD), 1) == _iota((r, NEW_PAD), 0) // nh).astype(BF16)
        gq = _sel_dot(pick, cn)
        hsel = _iota((r, nh), 1) == _iota((r, nh), 0) % nh
        cnq[...] = jnp.sum(jnp.where(hsel, gq, 0.0), axis=1, keepdims=True)
        kn = _dot(kvn_ref[0][:, :FOX_HD].astype(BF16), _place(FOX_HD, kq, 0))
        for i, part in enumerate(_split3(-cn)):
            kn = kn + _dot(part, _place(nh, kq, FOX_HD + hp * i))
        knew[...] = kn.astype(BF16)
        carry_s[...] = jnp.zeros(carry_s.shape, F32)
        _flash_init(m_ref, l_ref, acc_ref)

    x_all = lfbuf[slot]
    upper = (_iota((PAGE_SIZE, PAGE_SIZE), 0) > _iota((PAGE_SIZE, PAGE_SIZE), 1)).astype(BF16)
    suf_s[...] = _dot_sel(x_all, upper)
    ntile = cpages // ptile

    def tile(i, carry):
        for u in range(npar):
            ti = ntile - 1 - (i * npar + u)
            sp = [None] * ptile
            for jj in reversed(range(ptile)):
                row0 = pl.multiple_of((ti * ptile + jj) * nh, nh)
                sl = suf_s[pl.ds(row0, nh), :]
                sp[jj] = sl + carry
                carry = carry + sl[:, 0:1] + lfbuf[slot, pl.ds(row0, nh), 0:1]
            parts = [_pad_rows(p, hp) for p in _split3(jnp.concatenate(sp, axis=1))]
            starts = [pl.multiple_of((ti * ptile + jj) * PAGE_SIZE, PAGE_SIZE) for jj in range(ptile)]
            k_t = _lane_cat(kvbuf, slot, starts, FOX_HD).astype(BF16)
            v_t = _lane_cat(kvbuf, slot, [s + FOX_HD for s in starts], FOX_HD).astype(BF16)
            s = _dot(qaug[...], jnp.concatenate([k_t] + parts, axis=0)) + cnq[...]
            _flash_update(s, v_t, m_ref, l_ref, acc_ref, u, v_transposed=True)
        return carry

    carry_s[...] = lax.fori_loop(0, ntile // npar, tile, carry_s[...])

    @pl.when(c == nchunk - 1)
    def _fin():
        for u in range(1, npar):
            _flash_merge(m_ref, l_ref, acc_ref, 0, u)
        s = _dot_nt(qaug[...], knew[...]) + cnq[...]
        qi = _iota((r, NEW_PAD), 0) // nh
        kj = _iota((r, NEW_PAD), 1)
        vn = kvn_ref[0][:, FOX_HD:].astype(BF16)
        _flash_update(s, vn, m_ref, l_ref, acc_ref, 0, (kj <= qi) & (kj < ts))
        o_ref[0] = _flash_out(l_ref, acc_ref, 0).astype(o_ref.dtype)


def _paged_call(kern, n_steps, in_blocks, pools, out_block, out_shape, scratch, name):
    grid_spec = pltpu.PrefetchScalarGridSpec(
        num_scalar_prefetch=1,
        grid=(n_steps,),
        in_specs=in_blocks + [pl.BlockSpec(memory_space=pl.ANY)] * pools,
        out_specs=out_block,
        scratch_shapes=scratch)
    return pl.pallas_call(kern, grid_spec=grid_spec, out_shape=out_shape,
                          compiler_params=_cparams(("arbitrary",)), name=name)


def _tile_plan(cpages, max_par):
    ptile = max(p for p in (8, 4, 2, 1) if cpages % p == 0)
    npar = max(n for n in (4, 2, 1) if n <= max_par and (cpages // ptile) % n == 0)
    return ptile, npar


def _pad_new(x):
    return jnp.pad(x, ((0, 0), (0, NEW_PAD - x.shape[1]), (0, 0)))


def fox_sample(layer, page_table, fq, fkv_new, logf_new, kv_pool, logf_pool):
    bs, r, _ = fq.shape
    ts = fkv_new.shape[1]
    nh = r // ts
    npages = page_table.shape[1]
    nchunk = 1
    cpages = npages // nchunk
    ptile, npar = _tile_plan(cpages, 4)
    n_steps = bs * nchunk
    kq = FOX_HD + 3 * _pad_to(nh, 16)
    per_b = lambda w: pl.BlockSpec((1,) + w, lambda g, pt: (g // nchunk, 0, 0))
    kern = functools.partial(_fox_sample_kernel, layer=layer, nh=nh, ts=ts, nchunk=nchunk,
                             cpages=cpages, ptile=ptile, npar=npar, n_steps=n_steps)
    scratch = [pltpu.VMEM((2, cpages * PAGE_SIZE, LANE), F32), pltpu.VMEM((2, cpages * nh, LANE), F32),
               pltpu.SemaphoreType.DMA((2, 2)),
               pltpu.VMEM((r, kq), BF16), pltpu.VMEM((r, 1), F32), pltpu.VMEM((NEW_PAD, kq), BF16),
               pltpu.VMEM((cpages * nh, LANE), F32), pltpu.VMEM((nh, 1), F32),
               pltpu.VMEM((npar, r, 1), F32), pltpu.VMEM((npar, r, 1), F32), pltpu.VMEM((npar, r, FOX_HD), F32)]
    return _paged_call(
        kern, n_steps, [per_b((r, FOX_HD)), per_b((NEW_PAD, LANE)), per_b((NEW_PAD, nh))], 2,
        per_b((r, FOX_HD)), jax.ShapeDtypeStruct((bs, r, FOX_HD), BF16), scratch, "fox_sample",
    )(page_table, fq, _pad_new(fkv_new), _pad_new(logf_new), kv_pool, logf_pool)


def _mla_sample_kernel(pt_ref, ql_ref, qr_ref, ckvn_ref, krn_ref, ckvpool, krpool, o_ref,
                       ckvbuf, krbuf, sems, qlat, qrope, m_ref, l_ref, acc_ref,
                       *, layer, nh, ts, nchunk, cpages, ptile, npar, n_steps):
    g = pl.program_id(0)
    c = g % nchunk
    r = ts * nh
    scale = (QK_NOPE + QK_ROPE) ** -0.5

    def issue(step, slot, start):
        sb = step // nchunk
        p0 = (step % nchunk) * cpages
        _pages_dma(ckvpool, layer, pt_ref, sb, p0, cpages, ckvbuf, slot, sems.at[0], start)
        _pages_dma(krpool, layer, pt_ref, sb, p0, cpages, krbuf, slot, sems.at[1], start, QK_ROPE)

    slot = _paged_pipeline(g, n_steps, issue)

    @pl.when(c == 0)
    def _prep():
        qlat[...] = (ql_ref[0] * scale).astype(BF16)
        qrope[...] = (qr_ref[0] * scale).astype(BF16)
        _flash_init(m_ref, l_ref, acc_ref)

    def tile(i, cy):
        for u in range(npar):
            ti = i * npar + u
            r0 = pl.multiple_of(ti * ptile * PAGE_SIZE, ptile * PAGE_SIZE)
            ckv = ckvbuf[slot, pl.ds(r0, ptile * PAGE_SIZE), :].astype(BF16)
            starts = [pl.multiple_of((ti * ptile + jj) * QK_ROPE, QK_ROPE) for jj in range(ptile)]
            kr_t = _lane_cat(krbuf, slot, starts, QK_ROPE).astype(BF16)
            s = _dot_nt(qlat[...], ckv) + _dot(qrope[...], kr_t)
            _flash_update(s, ckv, m_ref, l_ref, acc_ref, u)
        return cy

    lax.fori_loop(0, cpages // (ptile * npar), tile, 0)

    @pl.when(c == nchunk - 1)
    def _fin():
        for u in range(1, npar):
            _flash_merge(m_ref, l_ref, acc_ref, 0, u)
        ckv = ckvn_ref[0].astype(BF16)
        s = _dot_nt(qlat[...], ckv) + _dot_nt(qrope[...], krn_ref[0].astype(BF16))
        qi = _iota((r, NEW_PAD), 0) // nh
        kj = _iota((r, NEW_PAD), 1)
        _flash_update(s, ckv, m_ref, l_ref, acc_ref, 0, (kj <= qi) & (kj < ts))
        o_ref[0] = _flash_out(l_ref, acc_ref, 0).astype(o_ref.dtype)


def mla_sample(layer, page_table, q_lat, q_rope, ckv_new, kr_new, ckv_pool, kr_pool):
    bs, r, c = q_lat.shape
    ts = ckv_new.shape[1]
    nh = r // ts
    npages = page_table.shape[1]
    nchunk = 2 if npages % 2 == 0 and npages >= 16 else 1
    cpages = npages // nchunk
    ptile, npar = _tile_plan(cpages, 2)
    n_steps = bs * nchunk
    per_b = lambda w: pl.BlockSpec((1,) + w, lambda g, pt: (g // nchunk, 0, 0))
    kern = functools.partial(_mla_sample_kernel, layer=layer, nh=nh, ts=ts, nchunk=nchunk,
                             cpages=cpages, ptile=ptile, npar=npar, n_steps=n_steps)
    scratch = [pltpu.VMEM((2, cpages * PAGE_SIZE, c), F32), pltpu.VMEM((2, cpages * QK_ROPE, LANE), F32),
               pltpu.SemaphoreType.DMA((2, 2)),
               pltpu.VMEM((r, c), BF16), pltpu.VMEM((r, QK_ROPE), BF16),
               pltpu.VMEM((npar, r, 1), F32), pltpu.VMEM((npar, r, 1), F32), pltpu.VMEM((npar, r, c), F32)]
    return _paged_call(
        kern, n_steps,
        [per_b((r, c)), per_b((r, QK_ROPE)), per_b((NEW_PAD, c)), per_b((NEW_PAD, QK_ROPE))], 2,
        per_b((r, c)), jax.ShapeDtypeStruct((bs, r, c), BF16), scratch, "mla_sample",
    )(page_table, q_lat, q_rope, _pad_new(ckv_new), _pad_new(kr_new), ckv_pool, kr_pool)


def _bucket_np(dist):
    n = np.maximum(dist, 0)
    max_exact = NUM_BUCKETS // 2
    large = max_exact + (np.log(np.maximum(n, 1).astype(np.float32) / max_exact)
                         / math.log(MAX_DIST / max_exact) * (NUM_BUCKETS - max_exact)).astype(np.int32)
    return np.where(n < max_exact, n, np.minimum(large, NUM_BUCKETS - 1)).astype(np.int32)


def _bias_kernel(tab_ref, bk_ref, o_ref):
    h = pl.program_id(0)
    bk = bk_ref[...]
    acc = jnp.zeros(bk.shape, F32)
    for b in range(NUM_BUCKETS):
        acc = jnp.where(bk == b, tab_ref[b, h], acc)
    o_ref[0] = acc


def bias_lookup(table, bucket):
    nh = table.shape[1]
    m, n = bucket.shape
    tm = _row_tile(m, 256) if m % 8 == 0 else m
    return pl.pallas_call(
        _bias_kernel,
        grid=(nh, m // tm),
        in_specs=[pl.BlockSpec(memory_space=pltpu.SMEM), pl.BlockSpec((tm, n), lambda i, j: (j, 0))],
        out_specs=pl.BlockSpec((1, tm, n), lambda i, j: (i, j, 0)),
        out_shape=jax.ShapeDtypeStruct((nh, m, n), F32),
        compiler_params=_cparams(("parallel", "parallel")),
        name="bias_lookup",
    )(table.astype(F32), jnp.asarray(bucket))


def _overlap_np(ncp, nsp, nc, ns):
    i = np.arange(ncp)[:, None] * CMP_D
    j = np.arange(nsp)[None, :] * SLC_B
    ov = np.maximum(np.minimum(i + CMP_L, j + SLC_B) - np.maximum(i, j), 0) / CMP_L
    ov = np.where((np.arange(ncp)[:, None] < nc) & (np.arange(nsp)[None, :] < ns), ov, 0.0)
    return ov.astype(np.float32)


def _pool_page(x, wcmp, h1, h2, row0):
    per = PAGE_SIZE // CMP_D
    g = (_iota((per, PAGE_SIZE), 1) // CMP_D == _iota((per, PAGE_SIZE), 0)).astype(BF16)
    wa = jnp.concatenate([wcmp[0:CMP_D]] * per, axis=0)
    wb = jnp.concatenate([wcmp[CMP_D:CMP_L]] * per, axis=0)
    h1[pl.ds(row0, per), :] = _sel_dot(g, x * wa)
    h2[pl.ds(row0, per), :] = _sel_dot(g, x * wb)


def _cmp_softmax(s, mask):
    s = jnp.where(mask, s, NEG)
    m = jnp.max(s, axis=-1, keepdims=True)
    p = jnp.where(mask, jnp.exp(s - m), 0.0)
    l = jnp.sum(p, axis=-1, keepdims=True)
    return p * (1.0 / jnp.where(l == 0.0, 1.0, l))


def _nsa_cmp_prompt_kernel(q_ref, x_ref, w_ref, bias_ref, ov_ref, oc_ref, imp_ref, kvc, h1, h2,
                           *, nh, t, nc, ncp):
    qi = pl.program_id(1)
    tq = TQ
    scale = NSA_HD ** -0.5
    per = PAGE_SIZE // CMP_D

    @pl.when(qi == 0)
    def _pool():
        h1[...] = jnp.zeros(h1.shape, F32)
        h2[...] = jnp.zeros(h2.shape, F32)
        w = w_ref[...]

        def body(c, cy):
            r0 = pl.multiple_of(c * PAGE_SIZE, PAGE_SIZE)
            _pool_page(x_ref[0, pl.ds(r0, PAGE_SIZE), :], w, h1, h2, pl.multiple_of(c * per, per))
            return cy

        lax.fori_loop(0, t // PAGE_SIZE, body, 0)
        kvc[...] = (h1[pl.ds(0, ncp), :] + h2[pl.ds(1, ncp), :]).astype(BF16)

    q = q_ref[0]
    kc = kvc[...]
    pos = qi * tq + _iota((tq, ncp), 0)
    n = _iota((tq, ncp), 1)
    mask = (n * CMP_D + CMP_L - 1 <= pos) & (n < nc)
    psum = jnp.zeros((tq, ncp), F32)
    for h in range(nh):
        qh = (_head_block(q, h, NSA_HD) * scale).astype(BF16)
        p = _cmp_softmax(_dot_nt(qh, kc) + bias_ref[h], mask)
        oc_ref[0, :, h * LANE:(h + 1) * LANE] = _dot(p.astype(BF16), kc)
        psum = psum + p
    imp_ref[0] = _dot_sel(psum, ov_ref[...])


def nsa_cmp_prompt(nq, ncmp, wcmp, bias_c, overlap, nc):
    b, t, _ = nq.shape
    nh, _, ncp = bias_c.shape
    nsp = overlap.shape[1]
    hb = t // CMP_D
    return pl.pallas_call(
        functools.partial(_nsa_cmp_prompt_kernel, nh=nh, t=t, nc=nc, ncp=ncp),
        grid=(b, t // TQ),
        in_specs=[pl.BlockSpec((1, TQ, nh * NSA_HD), lambda i, j: (i, j, 0)),
                  pl.BlockSpec((1, t, LANE), lambda i, j: (i, 0, 0)),
                  pl.BlockSpec((CMP_L, LANE), lambda i, j: (0, 0)),
                  pl.BlockSpec((nh, TQ, ncp), lambda i, j: (0, j, 0)),
                  pl.BlockSpec((ncp, nsp), lambda i, j: (0, 0))],
        out_specs=[pl.BlockSpec((1, TQ, nh * LANE), lambda i, j: (i, j, 0)),
                   pl.BlockSpec((1, TQ, nsp), lambda i, j: (i, j, 0))],
        out_shape=[jax.ShapeDtypeStruct((b, t, nh * LANE), F32), jax.ShapeDtypeStruct((b, t, nsp), F32)],
        scratch_shapes=[pltpu.VMEM((ncp, LANE), BF16), pltpu.VMEM((max(hb, ncp) + 8, LANE), F32),
                        pltpu.VMEM((max(hb, ncp) + 8, LANE), F32)],
        compiler_params=_cparams(("arbitrary", "arbitrary")),
        name="nsa_cmp_prompt",
    )(nq, ncmp, wcmp, bias_c, overlap)


def _nsa_cmp_sample_kernel(pt_ref, q_ref, w_ref, bias_ref, ov_ref, pool, oc_ref, imp_ref,
                           buf, sems, kvc, h1, h2, *, layer, nh, ts, npages, nc, ncp, n_steps, past_len):
    g = pl.program_id(0)
    r = ts * nh
    scale = NSA_HD ** -0.5
    per = PAGE_SIZE // CMP_D

    def issue(step, slot, start):
        _pages_dma(pool, layer, pt_ref, step, 0, npages, buf, slot, sems, start)

    slot = _paged_pipeline(g, n_steps, issue)
    h2[...] = jnp.zeros(h2.shape, F32)
    w1, w2 = w_ref[0], w_ref[1]
    gsel = (_iota((per, PAGE_SIZE), 1) // CMP_D == _iota((per, PAGE_SIZE), 0)).astype(BF16)
    unroll = max(u for u in (8, 4, 2, 1) if npages % u == 0)

    def body(c, cy):
        for u in range(unroll):
            pg = c * unroll + u
            xt = buf[slot, pl.ds(pl.multiple_of(pg * PAGE_SIZE, PAGE_SIZE), PAGE_SIZE), :]
            row0 = pl.multiple_of(pg * per, per)
            h1[pl.ds(row0, per), :] = _dot_nt(gsel, (xt * w1).astype(BF16))
            h2[pl.ds(row0, per), :] = _dot_nt(gsel, (xt * w2).astype(BF16))
        return cy

    lax.fori_loop(0, npages // unroll, body, 0)
    kvc[...] = (h1[pl.ds(0, ncp), :] + h2[pl.ds(1, ncp), :]).astype(BF16)
    kc = kvc[...]
    qp = _dot((q_ref[0] * scale).astype(BF16), _place(NSA_HD, LANE, 0)).astype(BF16)
    pos = past_len + _iota((r, ncp), 0) // nh
    n = _iota((r, ncp), 1)
    mask = (n * CMP_D + CMP_L - 1 <= pos) & (n < nc)
    p = _cmp_softmax(_dot_nt(qp, kc) + bias_ref[...], mask)
    oc_ref[0] = _dot(p.astype(BF16), kc)
    rows = (_iota((ts, r), 1) // nh == _iota((ts, r), 0)).astype(BF16)
    imp_ref[0] = _dot_sel(_sel_dot(rows, p), ov_ref[...])


def nsa_cmp_sample(layer, page_table, nq, wcmp, bias_c, overlap, pool, ts, nc):
    bs, r, _ = nq.shape
    nh = r // ts
    npages = page_table.shape[1]
    ncp = bias_c.shape[1]
    nsp = overlap.shape[1]
    hb = npages * (PAGE_SIZE // CMP_D)
    assert ncp == hb
    kern = functools.partial(_nsa_cmp_sample_kernel, layer=layer, nh=nh, ts=ts, npages=npages, nc=nc,
                             ncp=ncp, n_steps=bs, past_len=npages * PAGE_SIZE)
    per_b = lambda w: pl.BlockSpec((1,) + w, lambda g, pt: (g, 0, 0))
    const = lambda s: pl.BlockSpec(s, lambda g, pt: (0, 0))
    grid_spec = pltpu.PrefetchScalarGridSpec(
        num_scalar_prefetch=1, grid=(bs,),
        in_specs=[per_b((r, NSA_HD)), pl.BlockSpec((2, LANE, PAGE_SIZE), lambda g, pt: (0, 0, 0)),
                  const((r, ncp)), const((ncp, nsp)), pl.BlockSpec(memory_space=pl.ANY)],
        out_specs=[per_b((r, LANE)), per_b((ts, nsp))],
        scratch_shapes=[pltpu.VMEM((2, npages * PAGE_SIZE, LANE), F32), pltpu.SemaphoreType.DMA((2,)),
                        pltpu.VMEM((ncp, LANE), BF16), pltpu.VMEM((hb + 8, LANE), F32),
                        pltpu.VMEM((hb + 8, LANE), F32)])
    return pl.pallas_call(
        kern, grid_spec=grid_spec,
        out_shape=[jax.ShapeDtypeStruct((bs, r, LANE), F32), jax.ShapeDtypeStruct((bs, ts, nsp), F32)],
        compiler_params=_cparams(("arbitrary",)), name="nsa_cmp_sample",
    )(page_table, nq, wcmp, bias_c, overlap, pool)


def _topk_kernel(imp_ref, pos_ref, o_ref, score, *, ns, n_sel):
    imp = imp_ref[...]
    pos = pos_ref[...]
    blk = _iota(imp.shape, 0)
    cur = pos // SLC_B
    forced = (blk == 0) | (blk == cur) | (blk == cur - 1)
    sc = jnp.where(forced, BIG, jnp.where(blk * SLC_B <= pos, imp, -BIG))
    sc = jnp.where(blk < ns, sc, -3e38)
    score[...] = sc

    def body(j, rank):
        sj = score[pl.ds(j, 1), :]
        tie = jnp.where(blk > j, 1.0, 0.0)
        return rank + jnp.where(sj > sc, 1.0, jnp.where(sj == sc, tie, 0.0))

    rank = lax.fori_loop(0, ns, body, jnp.zeros(imp.shape, F32))
    o_ref[...] = jnp.where(rank < n_sel, 1.0, 0.0)


def topk_blocks(imp, pos, ns):
    n, nsp = imp.shape
    tn = LANE if n % LANE == 0 else n
    sel_t = pl.pallas_call(
        functools.partial(_topk_kernel, ns=ns, n_sel=min(N_SEL, ns)),
        grid=(n // tn,),
        in_specs=[pl.BlockSpec((nsp, tn), lambda i: (0, i)), pl.BlockSpec((1, tn), lambda i: (0, i))],
        out_specs=pl.BlockSpec((nsp, tn), lambda i: (0, i)),
        out_shape=jax.ShapeDtypeStruct((nsp, n), F32),
        scratch_shapes=[pltpu.VMEM((nsp, tn), F32)],
        compiler_params=_cparams(("parallel",)),
        name="topk_blocks",
    )(imp.T, pos.reshape(1, n).astype(jnp.int32))
    return sel_t


def _nsa_main_prompt_kernel(q_ref, gate_ref, oc_ref, sel_ref, slc_ref, win_ref, bt_ref, o_ref,
                            slcb, winb, qs, selm, m_ref, l_ref, acc_ref, *, nh, t, nsp):
    qi = pl.program_id(1)
    tq = tk = TQ
    scale = NSA_HD ** -0.5

    @pl.when(qi == 0)
    def _cast():
        slcb[...] = slc_ref[0].astype(BF16)
        winb[...] = win_ref[0].astype(BF16)

    q = q_ref[0]
    for h in range(nh):
        qs[h] = (_head_block(q, h, NSA_HD) * scale).astype(BF16)
    expand = (_iota((nsp, t), 1) // SLC_B == _iota((nsp, t), 0)).astype(BF16)
    selm[...] = _dot(sel_ref[0].astype(BF16), expand)
    _flash_init(m_ref, l_ref, acc_ref)
    dq = _iota((tq, tk), 0) - _iota((tq, tk), 1)

    def step(kt, cy):
        r0 = pl.multiple_of(kt * tk, tk)
        dist = (qi - kt) * tk + dq
        b0 = pl.multiple_of(jnp.minimum(qi - kt, 2) * tk, tk)
        kc = slcb[pl.ds(r0, tk), :]
        mask_s = jnp.where(dist >= 0, selm[:, pl.ds(r0, tk)], 0.0) > 0.5
        for h in range(nh):
            s = _dot_nt(qs[h], kc) + bt_ref[h, pl.ds(b0, tk), :]
            _flash_update(s, kc, m_ref, l_ref, acc_ref, h, mask_s)

        @pl.when((qi - kt) * tk < WINDOW + tq)
        def _win():
            kw = winb[pl.ds(r0, tk), :]
            mask_w = jnp.where(dist >= 0, dist, WINDOW) < WINDOW
            for h in range(nh):
                s = _dot_nt(qs[h], kw) + bt_ref[h, pl.ds(b0, tk), :]
                _flash_update(s, kw, m_ref, l_ref, acc_ref, nh + h, mask_w)
        return cy

    lax.fori_loop(0, qi + 1, step, 0)
    g = jax.nn.sigmoid(gate_ref[0])
    for h in range(nh):
        o = (g[:, 3 * h:3 * h + 1] * oc_ref[0, :, h * LANE:(h + 1) * LANE]
             + g[:, 3 * h + 1:3 * h + 2] * _flash_out(l_ref, acc_ref, h)
             + g[:, 3 * h + 2:3 * h + 3] * _flash_out(l_ref, acc_ref, nh + h))
        o_ref[0, :, h * NSA_HD:(h + 1) * NSA_HD] = o[:, NSA_HD:].astype(o_ref.dtype)


def nsa_main_prompt(nq, gates, o_c, sel, nslc, nwin, btiles):
    b, t, _ = nq.shape
    nh = btiles.shape[0]
    nsp = sel.shape[-1]
    qblk = lambda w: pl.BlockSpec((1, TQ, w), lambda i, j: (i, j, 0))
    full = lambda w: pl.BlockSpec((1, t, w), lambda i, j: (i, 0, 0))
    return pl.pallas_call(
        functools.partial(_nsa_main_prompt_kernel, nh=nh, t=t, nsp=nsp),
        grid=(b, t // TQ),
        in_specs=[qblk(nh * NSA_HD), qblk(LANE), qblk(nh * LANE), qblk(nsp), full(LANE), full(LANE),
                  pl.BlockSpec(btiles.shape, lambda i, j: (0, 0, 0))],
        out_specs=qblk(nh * NSA_HD),
        out_shape=jax.ShapeDtypeStruct((b, t, nh * NSA_HD), BF16),
        scratch_shapes=[pltpu.VMEM((t, LANE), BF16), pltpu.VMEM((t, LANE), BF16),
                        pltpu.VMEM((nh, TQ, LANE), BF16), pltpu.VMEM((TQ, t), F32),
                        pltpu.VMEM((2 * nh, TQ, 1), F32), pltpu.VMEM((2 * nh, TQ, 1), F32),
                        pltpu.VMEM((2 * nh, TQ, LANE), F32)],
        compiler_params=_cparams(("arbitrary", "arbitrary")),
        name="nsa_main_prompt",
    )(nq, gates, o_c, sel, nslc, nwin, btiles)


def _nsa_main_sample_kernel(pt_ref, fl_ref, q_ref, gate_ref, oc_ref, sel_ref, slcn_ref, win_ref, winn_ref,
                            blast_ref, bnew_ref, bwin_ref, bfar_ref, pool, o_ref,
                            buf, sems, plist, pcnt, qp, selr, m_ref, l_ref, acc_ref,
                            *, layer, nh, ts, npages, maxp, ptile, n_steps, past_len, wb):
    g = pl.program_id(0)
    r = ts * nh
    scale = NSA_HD ** -0.5
    hd = NSA_HD

    def page_copy(pg, k, slot):
        return pltpu.make_async_copy(
            pool.at[layer, pg], buf.at[slot, pl.ds(pl.multiple_of(k * PAGE_SIZE, PAGE_SIZE), PAGE_SIZE)],
            sems.at[slot])

    def issue(step, slot):
        def body(j, cnt):
            take = jnp.logical_and(fl_ref[step, j] != 0, cnt < maxp)

            @pl.when(take)
            def _():
                page_copy(pt_ref[step, j], cnt, slot).start()
                plist[slot, cnt] = j
            return cnt + take.astype(jnp.int32)
        pcnt[slot] = lax.fori_loop(0, npages, body, 0)

    slot = g % 2

    @pl.when(g == 0)
    def _():
        buf[...] = jnp.zeros(buf.shape, F32)
        for s_ in range(2):
            for k in range(maxp):
                plist[s_, k] = 0
        issue(g, slot)

    @pl.when(g + 1 < n_steps)
    def _():
        issue(g + 1, 1 - slot)

    cnt = pcnt[slot]

    def wait_body(k, cy):
        page_copy(0, k, slot).wait()
        return cy

    lax.fori_loop(0, cnt, wait_body, 0)

    qp[...] = (q_ref[0] * scale).astype(BF16)
    rows = (_iota((r, ts), 1) == _iota((r, ts), 0) // nh).astype(BF16)
    selr[...] = _dot(rows, sel_ref[0].astype(BF16)).astype(BF16)
    _flash_init(m_ref, l_ref, acc_ref)
    q = qp[...]

    def tile(i, cy):
        masks, biases, starts = [], [], []
        for jj in range(ptile):
            k = i * ptile + jj
            j = plist[slot, k]
            base = pl.multiple_of((2 * j // LANE) * LANE, LANE)
            expand = (_iota((LANE, PAGE_SIZE), 0) == (2 * j) % LANE + _iota((LANE, PAGE_SIZE), 1) // SLC_B)
            hit = _dot(selr[:, pl.ds(base, LANE)], expand.astype(BF16))
            masks.append(jnp.where(k < cnt, hit, 0.0) > 0.5)
            biases.append(jnp.where(j == npages - 1, blast_ref[...], bfar_ref[...]) + jnp.zeros((r, PAGE_SIZE), F32))
            starts.append(pl.multiple_of(k * PAGE_SIZE, PAGE_SIZE))
        k_t = _lane_cat(buf, slot, starts, hd).astype(BF16)
        v_t = _lane_cat(buf, slot, [s + hd for s in starts], hd).astype(BF16)
        s = _dot(q, k_t) + jnp.concatenate(biases, axis=1)
        _flash_update(s, v_t, m_ref, l_ref, acc_ref, 0, jnp.concatenate(masks, axis=1), v_transposed=True)
        return cy

    lax.fori_loop(0, (cnt + ptile - 1) // ptile, tile, 0)

    qi = _iota((r, NEW_PAD), 0) // nh
    kj = _iota((r, NEW_PAD), 1)
    new_mask = (kj <= qi) & (kj < ts)
    kn = slcn_ref[0][:, :hd].astype(BF16)
    vn = slcn_ref[0][:, hd:].astype(BF16)
    _flash_update(_dot_nt(q, kn) + bnew_ref[...], vn, m_ref, l_ref, acc_ref, 0, new_mask)
    kw = win_ref[0, 0, :hd, :].astype(BF16)
    vw = win_ref[0, 0, hd:, :].astype(BF16)
    dist = wb + _iota((r, wb), 0) // nh - _iota((r, wb), 1)
    _flash_update(_dot(q, kw) + bwin_ref[:, :wb], vw, m_ref, l_ref, acc_ref, 1, dist < WINDOW, v_transposed=True)
    kn = winn_ref[0][:, :hd].astype(BF16)
    vn = winn_ref[0][:, hd:].astype(BF16)
    _flash_update(_dot_nt(q, kn) + bwin_ref[:, wb:], vn, m_ref, l_ref, acc_ref, 1, new_mask)
    gt = jax.nn.sigmoid(gate_ref[0])
    o = (gt[:, 0:1] * oc_ref[0][:, hd:] + gt[:, 1:2] * _flash_out(l_ref, acc_ref, 0)
         + gt[:, 2:3] * _flash_out(l_ref, acc_ref, 1))
    o_ref[0] = o.astype(o_ref.dtype)


def nsa_main_sample(layer, page_table, nq, gates, o_c, sel, slc_new, win_t, win_new, b_last, b_new, b_win, b_far,
                    pool, ts):
    bs, r, _ = nq.shape
    nh = r // ts
    npages = page_table.shape[1]
    wb = win_t.shape[-1]
    nsp = sel.shape[-1]
    ptile = 4 if npages % 4 == 0 else 1
    maxp = min(_pad_to(ts * min(N_SEL, nsp), ptile), npages)
    per_page = PAGE_SIZE // SLC_B
    flags = (jnp.max(sel[:, :, :npages * per_page].reshape(bs, ts, npages, per_page), axis=(1, 3)) > 0.5)
    kern = functools.partial(_nsa_main_sample_kernel, layer=layer, nh=nh, ts=ts, npages=npages, maxp=maxp,
                             ptile=ptile, n_steps=bs, past_len=npages * PAGE_SIZE, wb=wb)
    per_b = lambda w: pl.BlockSpec((1,) + w, lambda g, pt, fl: (g, 0, 0))
    const = lambda a: pl.BlockSpec(a.shape, lambda g, pt, fl: (0, 0))
    scratch = [pltpu.VMEM((2, maxp * PAGE_SIZE, LANE), F32), pltpu.SemaphoreType.DMA((2,)),
               pltpu.SMEM((2, maxp), jnp.int32), pltpu.SMEM((2,), jnp.int32),
               pltpu.VMEM((r, NSA_HD), BF16), pltpu.VMEM((r, nsp), BF16),
               pltpu.VMEM((2, r, 1), F32), pltpu.VMEM((2, r, 1), F32), pltpu.VMEM((2, r, NSA_HD), F32)]
    grid_spec = pltpu.PrefetchScalarGridSpec(
        num_scalar_prefetch=2, grid=(bs,),
        in_specs=[per_b((r, NSA_HD)), per_b((r, N_BRANCH)), per_b((r, LANE)), per_b((ts, nsp)),
                  per_b((NEW_PAD, LANE)),
                  pl.BlockSpec((1, 1, LANE, wb), lambda g, pt, fl: (layer, g, 0, 0)),
                  per_b((NEW_PAD, LANE)),
                  const(b_last), const(b_new), const(b_win), const(b_far),
                  pl.BlockSpec(memory_space=pl.ANY)],
        out_specs=per_b((r, NSA_HD)),
        scratch_shapes=scratch)
    return pl.pallas_call(
        kern, grid_spec=grid_spec, out_shape=jax.ShapeDtypeStruct((bs, r, NSA_HD), BF16),
        compiler_params=_cparams(("arbitrary",)), name="nsa_main_sample",
    )(page_table, flags.astype(jnp.int32), nq, gates, o_c, sel, _pad_new(slc_new), win_t, _pad_new(win_new),
      b_last, b_new, b_win, b_far, pool)


def _conv_prompt_kernel(g_ref, halo_ref, u_ref, w_ref, b_ref, o_ref, ext):
    ti = pl.program_id(1)
    tt = g_ref.shape[1]
    ext[0:8, :] = jnp.where(ti == 0, 0.0, halo_ref[0])
    ext[8:, :] = g_ref[0]
    c = (b_ref[...] + ext[pl.ds(8, tt), :] * w_ref[2:3, :] + ext[pl.ds(7, tt), :] * w_ref[1:2, :]
         + ext[pl.ds(6, tt), :] * w_ref[0:1, :])
    o_ref[0] = (c * jax.nn.sigmoid(c) * u_ref[0]).astype(o_ref.dtype)


def conv_act_prompt(g, u, w_conv, b_conv):
    b, t, f = g.shape
    tt = _row_tile(t, 512)
    tf = max(x for x in range(LANE, min(f, 2048) + 1, LANE) if f % x == 0)
    hb = tt // 8
    blk = pl.BlockSpec((1, tt, tf), lambda i, j, k: (i, j, k))
    vec = lambda n: pl.BlockSpec((n, tf), lambda i, j, k: (0, k))
    return pl.pallas_call(
        _conv_prompt_kernel,
        grid=(b, t // tt, f // tf),
        in_specs=[blk, pl.BlockSpec((1, 8, tf), lambda i, j, k: (i, jnp.maximum(j * hb - 1, 0), k)),
                  blk, vec(CONV_W), vec(1)],
        out_specs=blk,
        out_shape=jax.ShapeDtypeStruct((b, t, f), BF16),
        scratch_shapes=[pltpu.VMEM((tt + 8, tf), F32)],
        compiler_params=_cparams(("parallel", "parallel", "parallel")),
        name="conv_act_prompt",
    )(g, g, u, w_conv, b_conv.reshape(1, f))


def _conv_rows_kernel(g0_ref, g1_ref, g2_ref, u_ref, w_ref, b_ref, o_ref):
    c = b_ref[...] + g0_ref[...] * w_ref[0:1, :] + g1_ref[...] * w_ref[1:2, :] + g2_ref[...] * w_ref[2:3, :]
    o_ref[...] = (c * jax.nn.sigmoid(c) * u_ref[...]).astype(o_ref.dtype)


def conv_act_rows(g0, g1, g2, u, w_conv, b_conv):
    n, f = u.shape
    tf = max(x for x in range(LANE, min(f, 2048) + 1, LANE) if f % x == 0)
    blk = pl.BlockSpec((n, tf), lambda k: (0, k))
    vec = lambda m: pl.BlockSpec((m, tf), lambda k: (0, k))
    return pl.pallas_call(
        _conv_rows_kernel,
        grid=(f // tf,),
        in_specs=[blk, blk, blk, blk, vec(CONV_W), vec(1)],
        out_specs=blk,
        out_shape=jax.ShapeDtypeStruct((n, f), BF16),
        compiler_params=_cparams(("parallel",)),
        name="conv_act_rows",
    )(g0, g1, g2, u, w_conv, b_conv.reshape(1, f))


def _tflash_init(m_scr, l_scr, acc_scr):
    m_scr[...] = jnp.full(m_scr.shape, NEG, F32)
    l_scr[...] = jnp.zeros(l_scr.shape, F32)
    acc_scr[...] = jnp.zeros(acc_scr.shape, F32)


def _tflash_heads(st_scr, pt_scr, m_scr, l_scr, a_scr, slot, nh, tq, bias_fn=None, mask=None):
    for h in range(nh):
        hs = slice(h * tq, (h + 1) * tq)
        st = st_scr[:, hs]
        if bias_fn is not None:
            st = st + bias_fn(h)
        if mask is not None:
            st = jnp.where(mask, st, NEG)
        m_old = m_scr[slot, :, hs]
        m_new = jnp.maximum(m_old, jnp.max(st, axis=0, keepdims=True))
        alpha = jnp.exp(m_old - m_new)
        p = jnp.exp(st - m_new)
        l_scr[slot, :, hs] = alpha * l_scr[slot, :, hs] + jnp.sum(p, axis=0, keepdims=True)
        m_scr[slot, :, hs] = m_new
        a_scr[:, hs] = alpha
        pt_scr[:, hs] = p.astype(BF16)


def _tflash_acc(acc_scr, slot, a_scr, v_t, pt_scr):
    acc_scr[slot] = acc_scr[slot] * a_scr[...] + _dot(v_t, pt_scr[...])


def _tflash_out(l_scr, acc_scr, slot, h, tq):
    hs = slice(h * tq, (h + 1) * tq)
    l = l_scr[slot, :, hs]
    on = (acc_scr[slot, :, hs] * (1.0 / jnp.where(l == 0.0, 1.0, l))).astype(BF16)
    eye = (_iota((tq, tq), 0) == _iota((tq, tq), 1)).astype(BF16)
    return _dot_nt(eye, on)


def _transpose_cols(x_bf16, col0, ncols):
    w = x_bf16.shape[1]
    sel = (_iota((ncols, w), 1) == _iota((ncols, w), 0) + col0).astype(BF16)
    return _dot_nt(sel, x_bf16)


def _fox_prompt_t_kernel(q_ref, kv_ref, logf_ref, o_ref, kcat, v_t, cum, qs, st_scr, pt_scr,
                         m_scr, l_scr, a_scr, acc_scr, *, nh, t, tk):
    qi = pl.program_id(1)
    tq = TQ
    hp = _pad_to(nh, 16)
    one0 = FOX_HD + 3 * hp
    scale = FOX_HD ** -0.5

    @pl.when(qi == 0)
    def _build():
        low = (_iota((tk, tk), 1) <= _iota((tk, tk), 0)).astype(BF16)
        lane = _iota((tk, LANE), 1)
        ones = jnp.where((lane >= one0) & (lane < one0 + 3), 1.0, 0.0)

        def body(i, carry):
            r0 = pl.multiple_of(i * tk, tk)
            c = _sel_dot(low, logf_ref[0, pl.ds(r0, tk), :]) + carry
            cum[pl.ds(r0, tk), :] = c
            kvb = kv_ref[0, pl.ds(r0, tk), :].astype(BF16)
            keep_k = ((_iota((LANE, LANE), 0) == _iota((LANE, LANE), 1))
                      & (_iota((LANE, LANE), 0) < FOX_HD)).astype(BF16)
            kc = _dot(kvb, keep_k) + ones
            for j, part in enumerate(_split3(-c)):
                kc = kc + _dot(part, _place(nh, LANE, FOX_HD + hp * j))
            kcat[pl.ds(r0, tk), :] = kc.astype(BF16)
            v_t[:, pl.ds(r0, tk)] = _transpose_cols(kvb, FOX_HD, FOX_HD).astype(BF16)
            return c[tk - 1:tk, :]

        lax.fori_loop(0, t // tk, body, jnp.zeros((1, nh), F32))

    q = q_ref[0]
    lane = _iota((tq, LANE), 1)
    cqs = _split3(cum[pl.ds(pl.multiple_of(qi * tq, tq), tq), :])
    for h in range(nh):
        qh = _head_block(q, h, FOX_HD) * scale
        qh = jnp.where((lane >= FOX_HD) & (lane < one0) & ((lane - FOX_HD) % hp == h), 1.0, qh)
        for j in range(3):
            qh = jnp.where(lane == one0 + j, cqs[j][:, h:h + 1].astype(F32), qh)
        qs[pl.ds(h * tq, tq), :] = qh.astype(BF16)
    _tflash_init(m_scr, l_scr, acc_scr)
    qpos = qi * tq + _iota((tk, tq), 1)

    def step(kt, masked):
        r0 = pl.multiple_of(kt * tk, tk)
        st_scr[...] = _dot_nt(kcat[pl.ds(r0, tk), :], qs[...])
        mask = (r0 + _iota((tk, tq), 0) <= qpos) if masked else None
        _tflash_heads(st_scr, pt_scr, m_scr, l_scr, a_scr, 0, nh, tq, mask=mask)
        _tflash_acc(acc_scr, 0, a_scr, v_t[:, pl.ds(r0, tk)], pt_scr)

    n_kt = ((qi + 1) * tq + tk - 1) // tk

    def loop_body(kt, carry):
        step(kt, False)
        return carry

    lax.fori_loop(0, n_kt - 1, loop_body, 0)
    step(n_kt - 1, True)
    per = LANE // FOX_HD
    for h0 in range(0, nh, per):
        o = jnp.concatenate([_tflash_out(l_scr, acc_scr, 0, h0 + u, tq) for u in range(per)], axis=1)
        o_ref[0, :, h0 * FOX_HD:(h0 + per) * FOX_HD] = o.astype(o_ref.dtype)


def fox_prompt_t(fq, fkv, logf):
    b, t, _ = fq.shape
    nh = logf.shape[-1]
    tk = min(256, t)
    r = nh * TQ
    assert FOX_HD + 3 * _pad_to(nh, 16) + 3 <= LANE
    return pl.pallas_call(
        functools.partial(_fox_prompt_t_kernel, nh=nh, t=t, tk=tk),
        grid=(b, t // TQ),
        in_specs=[pl.BlockSpec((1, TQ, nh * FOX_HD), lambda i, j: (i, j, 0)),
                  pl.BlockSpec((1, t, 2 * FOX_HD), lambda i, j: (i, 0, 0)),
                  pl.BlockSpec((1, t, nh), lambda i, j: (i, 0, 0))],
        out_specs=pl.BlockSpec((1, TQ, nh * FOX_HD), lambda i, j: (i, j, 0)),
        out_shape=jax.ShapeDtypeStruct((b, t, nh * FOX_HD), BF16),
        scratch_shapes=[pltpu.VMEM((t, LANE), BF16), pltpu.VMEM((FOX_HD, t), BF16), pltpu.VMEM((t, nh), F32),
                        pltpu.VMEM((r, LANE), BF16), pltpu.VMEM((tk, r), F32), pltpu.VMEM((tk, r), BF16),
                        pltpu.VMEM((1, 1, r), F32), pltpu.VMEM((1, 1, r), F32), pltpu.VMEM((1, r), F32),
                        pltpu.VMEM((1, FOX_HD, r), F32)],
        compiler_params=_cparams(("arbitrary", "arbitrary")),
        name="fox_prompt",
    )(fq, fkv, logf)


def _mla_prompt_t_kernel(ql_ref, qr_ref, ckv_ref, kr_ref, o_ref, kcat, v_t, qs, st_scr, pt_scr,
                         m_scr, l_scr, a_scr, acc_scr, *, nh, c, t, tk):
    qi = pl.program_id(1)
    tq = TQ
    scale = (QK_NOPE + QK_ROPE) ** -0.5

    @pl.when(qi == 0)
    def _build():
        def body(i, cy):
            r0 = pl.multiple_of(i * tk, tk)
            ckvb = ckv_ref[0, pl.ds(r0, tk), :].astype(BF16)
            kcat[pl.ds(r0, tk), :c] = ckvb
            kcat[pl.ds(r0, tk), c:] = _dot(kr_ref[0, pl.ds(r0, tk), :].astype(BF16),
                                           _place(QK_ROPE, LANE, 0)).astype(BF16)
            v_t[:, pl.ds(r0, tk)] = _transpose_cols(ckvb, 0, c).astype(BF16)
            return cy
        lax.fori_loop(0, t // tk, body, 0)

    qr = (qr_ref[0] * scale).astype(BF16)
    for h in range(nh):
        sel = (_iota((nh * QK_ROPE, LANE), 0) == _iota((nh * QK_ROPE, LANE), 1) + h * QK_ROPE).astype(BF16)
        qs[pl.ds(h * tq, tq), :c] = (ql_ref[0, :, h * c:(h + 1) * c] * scale).astype(BF16)
        qs[pl.ds(h * tq, tq), c:] = _dot(qr, sel).astype(BF16)
    _tflash_init(m_scr, l_scr, acc_scr)
    qpos = qi * tq + _iota((tk, tq), 1)

    def step(kt, masked):
        r0 = pl.multiple_of(kt * tk, tk)
        st_scr[...] = _dot_nt(kcat[pl.ds(r0, tk), :], qs[...])
        mask = (r0 + _iota((tk, tq), 0) <= qpos) if masked else None
        _tflash_heads(st_scr, pt_scr, m_scr, l_scr, a_scr, 0, nh, tq, mask=mask)
        _tflash_acc(acc_scr, 0, a_scr, v_t[:, pl.ds(r0, tk)], pt_scr)

    n_kt = ((qi + 1) * tq + tk - 1) // tk

    def loop_body(kt, carry):
        step(kt, False)
        return carry

    lax.fori_loop(0, n_kt - 1, loop_body, 0)
    step(n_kt - 1, True)
    for h in range(nh):
        o_ref[0, :, h * c:(h + 1) * c] = _tflash_out(l_scr, acc_scr, 0, h, tq).astype(o_ref.dtype)


def mla_prompt_t(q_lat, q_rope, ckv, krope):
    b, t, c = ckv.shape
    nh = q_lat.shape[-1] // c
    tk = min(256, t)
    r = nh * TQ
    return pl.pallas_call(
        functools.partial(_mla_prompt_t_kernel, nh=nh, c=c, t=t, tk=tk),
        grid=(b, t // TQ),
        in_specs=[pl.BlockSpec((1, TQ, nh * c), lambda i, j: (i, j, 0)),
                  pl.BlockSpec((1, TQ, nh * QK_ROPE), lambda i, j: (i, j, 0)),
                  pl.BlockSpec((1, t, c), lambda i, j: (i, 0, 0)),
                  pl.BlockSpec((1, t, QK_ROPE), lambda i, j: (i, 0, 0))],
        out_specs=pl.BlockSpec((1, TQ, nh * c), lambda i, j: (i, j, 0)),
        out_shape=jax.ShapeDtypeStruct((b, t, nh * c), BF16),
        scratch_shapes=[pltpu.VMEM((t, c + LANE), BF16), pltpu.VMEM((c, t), BF16),
                        pltpu.VMEM((r, c + LANE), BF16), pltpu.VMEM((tk, r), F32), pltpu.VMEM((tk, r), BF16),
                        pltpu.VMEM((1, 1, r), F32), pltpu.VMEM((1, 1, r), F32), pltpu.VMEM((1, r), F32),
                        pltpu.VMEM((1, c, r), F32)],
        compiler_params=_cparams(("arbitrary", "arbitrary")),
        name="mla_prompt",
    )(q_lat, q_rope, ckv, krope)


def _nsa_main_prompt_t_kernel(q_ref, gate_ref, oc_ref, selt_ref, slc_ref, win_ref, bt_ref, o_ref,
                              slck, slcv_t, wink, winv_t, qs, selm, st_scr, pt_scr,
                              m_scr, l_scr, a_scr, acc_scr, *, nh, t, nsp):
    qi = pl.program_id(1)
    tq = tk = TQ
    hd = NSA_HD
    scale = hd ** -0.5

    @pl.when(qi == 0)
    def _build():
        def body(i, cy):
            r0 = pl.multiple_of(i * tk, tk)
            sb = slc_ref[0, pl.ds(r0, tk), :].astype(BF16)
            wb_ = win_ref[0, pl.ds(r0, tk), :].astype(BF16)
            slck[pl.ds(r0, tk), :] = sb
            wink[pl.ds(r0, tk), :] = wb_
            slcv_t[:, pl.ds(r0, tk)] = _transpose_cols(sb, hd, hd).astype(BF16)
            winv_t[:, pl.ds(r0, tk)] = _transpose_cols(wb_, hd, hd).astype(BF16)
            return cy
        lax.fori_loop(0, t // tk, body, 0)

    q = q_ref[0]
    for h in range(nh):
        qs[pl.ds(h * tq, tq), :] = (_head_block(q, h, hd) * scale).astype(BF16)
    expand_t = (_iota((t, nsp), 0) // SLC_B == _iota((t, nsp), 1)).astype(BF16)
    selm[...] = _dot(expand_t, selt_ref[...].astype(BF16))
    _tflash_init(m_scr, l_scr, acc_scr)
    dq = _iota((tk, tq), 1) - _iota((tk, tq), 0)

    def step(kt, cy):
        r0 = pl.multiple_of(kt * tk, tk)
        dist = (qi - kt) * tk + dq
        b0 = pl.multiple_of(jnp.minimum(qi - kt, 2) * tk, tk)
        bias = lambda h: bt_ref[h, pl.ds(b0, tk), :]
        st_scr[...] = _dot_nt(slck[pl.ds(r0, tk), :], qs[...])
        mask_s = jnp.where(dist >= 0, selm[pl.ds(r0, tk), :], 0.0) > 0.5
        _tflash_heads(st_scr, pt_scr, m_scr, l_scr, a_scr, 0, nh, tq, bias, mask_s)
        _tflash_acc(acc_scr, 0, a_scr, slcv_t[:, pl.ds(r0, tk)], pt_scr)

        @pl.when((qi - kt) * tk < WINDOW + tq)
        def _win():
            st_scr[...] = _dot_nt(wink[pl.ds(r0, tk), :], qs[...])
            mask_w = jnp.where(dist >= 0, dist, WINDOW) < WINDOW
            _tflash_heads(st_scr, pt_scr, m_scr, l_scr, a_scr, 1, nh, tq, bias, mask_w)
            _tflash_acc(acc_scr, 1, a_scr, winv_t[:, pl.ds(r0, tk)], pt_scr)
        return cy

    lax.fori_loop(0, qi + 1, step, 0)
    g = jax.nn.sigmoid(gate_ref[0])
    per = LANE // hd
    for h0 in range(0, nh, per):
        outs = []
        for h in range(h0, h0 + per):
            outs.append(g[:, 3 * h:3 * h + 1] * oc_ref[0, :, h * LANE + hd:(h + 1) * LANE]
                        + g[:, 3 * h + 1:3 * h + 2] * _tflash_out(l_scr, acc_scr, 0, h, tq)
                        + g[:, 3 * h + 2:3 * h + 3] * _tflash_out(l_scr, acc_scr, 1, h, tq))
        o_ref[0, :, h0 * hd:(h0 + per) * hd] = jnp.concatenate(outs, axis=1).astype(o_ref.dtype)


def nsa_main_prompt_t(nq, gates, o_c, sel_t, nslc, nwin, btiles_t):
    b, t, _ = nq.shape
    nh = btiles_t.shape[0]
    nsp = sel_t.shape[0]
    r = nh * TQ
    nq_t = t // TQ
    qblk = lambda w: pl.BlockSpec((1, TQ, w), lambda i, j: (i, j, 0))
    full = lambda w: pl.BlockSpec((1, t, w), lambda i, j: (i, 0, 0))
    return pl.pallas_call(
        functools.partial(_nsa_main_prompt_t_kernel, nh=nh, t=t, nsp=nsp),
        grid=(b, nq_t),
        in_specs=[qblk(nh * NSA_HD), qblk(LANE), qblk(nh * LANE),
                  pl.BlockSpec((nsp, TQ), lambda i, j: (0, i * nq_t + j)), full(LANE), full(LANE),
                  pl.BlockSpec(btiles_t.shape, lambda i, j: (0, 0, 0))],
        out_specs=qblk(nh * NSA_HD),
        out_shape=jax.ShapeDtypeStruct((b, t, nh * NSA_HD), BF16),
        scratch_shapes=[pltpu.VMEM((t, LANE), BF16), pltpu.VMEM((NSA_HD, t), BF16),
                        pltpu.VMEM((t, LANE), BF16), pltpu.VMEM((NSA_HD, t), BF16),
                        pltpu.VMEM((r, LANE), BF16), pltpu.VMEM((t, TQ), F32),
                        pltpu.VMEM((TQ, r), F32), pltpu.VMEM((TQ, r), BF16),
                        pltpu.VMEM((2, 1, r), F32), pltpu.VMEM((2, 1, r), F32), pltpu.VMEM((1, r), F32),
                        pltpu.VMEM((2, NSA_HD, r), F32)],
        compiler_params=_cparams(("arbitrary", "arbitrary")),
        name="nsa_main_prompt",
    )(nq, gates, o_c, sel_t, nslc, nwin, btiles_t)


def kernel(x_prompt, x_sample, cache_fox_kv, cache_fox_logf, cache_mla_ckv, cache_mla_krope, cache_nsa_cmp_kv, cache_nsa_slc_kv, state_nsa_win_kv, state_conv, page_table, g_attn, w_in, b_fgate, g_qa, wq_b, g_kva, wkv_b, w_cmp, rel_bias_table, w_out, g_ffn, w_gate, w_up, w_conv, b_conv, w_down, g_final):
    b, t, d = x_prompt.shape
    bs, ts, _ = x_sample.shape
    depth = w_in.shape[0]
    nh = d // 256
    q_lora, kv_lora = g_qa.shape[1], g_kva.shape[1]
    past_len = page_table.shape[1] * PAGE_SIZE
    n_p, n_s = b * t, bs * ts
    lay = _in_layout(nh, q_lora, kv_lora)
    pos_p = jnp.arange(t, dtype=jnp.int32)
    pos_s = past_len + jnp.arange(ts, dtype=jnp.int32)
    pos_all = jnp.concatenate([jnp.tile(pos_p, b), jnp.tile(pos_s, bs)])
    cos_k, sin_k = _rope_tables(pos_all, LANE // QK_ROPE)
    cos_q, sin_q = _rope_tables(pos_all, nh)
    x = jnp.concatenate([x_prompt.reshape(n_p, d), x_sample.reshape(n_s, d)], axis=0)
    w_buf = state_nsa_win_kv.shape[2]
    f_ff = w_gate.shape[2]
    assert t % TQ == 0 and ts < CMP_D and ts <= NEW_PAD

    ratio = CMP_L // CMP_D
    nc_p = t // CMP_D - ratio + 1
    ncp_p = _pad_to(nc_p, LANE)
    ns_p = -(-t // SLC_B)
    nsp_p = _pad_to(ns_p, LANE)
    t_s = past_len + ts
    nc_s = t_s // CMP_D - ratio + 1
    ncp_s = (past_len // CMP_D)
    ns_s = -(-t_s // SLC_B)
    nsp_s = _pad_to(ns_s, LANE)
    tk_s = PAGE_SIZE
    ar = np.arange
    cmp_end = lambda n: ar(n) * CMP_D + CMP_L - 1
    d_tile = ar(TQ)[None, :] - ar(TQ)[:, None]
    bk_tiles = np.concatenate([_bucket_np(d_tile), _bucket_np(TQ + d_tile),
                               np.full((TQ, TQ), NUM_BUCKETS - 1, np.int32)], axis=0)
    bk_cmp_p = _bucket_np(ar(t)[:, None] - cmp_end(ncp_p)[None, :])
    ps = past_len + ar(ts)[:, None]
    col_w = ar(w_buf + NEW_PAD)[None, :]
    kpos_w = np.where(col_w < w_buf, past_len - w_buf + col_w, past_len + col_w - w_buf)
    bk_s = np.concatenate([_bucket_np(ps - cmp_end(ncp_s)[None, :]),
                           _bucket_np(ps - (past_len - tk_s + ar(tk_s))[None, :]),
                           _bucket_np(ar(ts)[:, None] - ar(NEW_PAD)[None, :]),
                           _bucket_np(ps - kpos_w)], axis=1)
    btiles = bias_lookup(rel_bias_table, bk_tiles)
    bias_c_p = bias_lookup(rel_bias_table, bk_cmp_p)
    bias_s = jnp.transpose(bias_lookup(rel_bias_table, bk_s), (1, 0, 2)).reshape(ts * nh, -1)
    cuts = np.cumsum([ncp_s, tk_s, NEW_PAD])
    bias_c_s, b_last, b_new, b_win = (bias_s[:, :cuts[0]], bias_s[:, cuts[0]:cuts[1]],
                                      bias_s[:, cuts[1]:cuts[2]], bias_s[:, cuts[2]:])
    b_far = jnp.tile(rel_bias_table[NUM_BUCKETS - 1].astype(F32), ts).reshape(ts * nh, 1)
    overlap_p = jnp.asarray(_overlap_np(ncp_p, nsp_p, nc_p, ns_p), BF16)
    overlap_s = jnp.asarray(_overlap_np(ncp_s, nsp_s, nc_s, ns_s), BF16)
    posr_p = jnp.tile(pos_p, b)
    posr_s = jnp.tile(pos_s, bs)

    pool_shape = cache_fox_kv.shape[:3]
    kv_t = lambda a: jnp.transpose(a, (0, 1, 3, 4, 2)).reshape(a.shape[:2] + (a.shape[3] * a.shape[4], a.shape[2]))
    fox_pool, cmp_pool, slc_pool = kv_t(cache_fox_kv), kv_t(cache_nsa_cmp_kv), kv_t(cache_nsa_slc_kv)
    win_t = kv_t(state_nsa_win_kv)
    logf_pool = jnp.transpose(cache_fox_logf, (0, 1, 3, 2))
    kr_pool = jnp.transpose(cache_mla_krope, (0, 1, 3, 2))

    states = []
    for l in range(depth):
        hn = rmsnorm_rows(x, g_attn[l], BF16)
        p = matmul(hn, _pad_w_in(w_in[l], lay))
        logf, cqn, ckv, krope = post_proj(p, b_fgate[l], g_qa[l], g_kva[l], cos_k, sin_k, lay)
        wq = jnp.concatenate([wq_b[l][:, :, :QK_NOPE].reshape(q_lora, -1),
                              wq_b[l][:, :, QK_NOPE:].reshape(q_lora, -1)], axis=1).astype(BF16)
        q_mla = matmul(cqn, wq)
        q_rope = q_rope_rows(q_mla, nh * QK_NOPE, cos_q, sin_q)
        w_uk = jnp.transpose(wkv_b[l][:, :, :QK_NOPE], (1, 2, 0)).astype(BF16)
        w_uv = jnp.transpose(wkv_b[l][:, :, QK_NOPE:], (1, 0, 2)).astype(BF16)
        q_lat = head_matmul(q_mla, w_uk, F32)
        seg = lambda name, width=None: p[:, lay[name][0]:lay[name][0] + (width or lay[name][1])]
        fq, fkv, nq = seg("fq"), seg("fkv"), seg("nq")
        ncmp, nslc, nwin = seg("ncmp"), seg("nslc"), seg("nwin")
        wcmp = jnp.concatenate([w_cmp[l][0], w_cmp[l][1]], axis=1).astype(F32)
        pr = lambda a: a[:n_p].reshape(b, t, -1)
        sm = lambda a: a[n_p:].reshape(bs, ts, -1)
        smh = lambda a, w: a[n_p:].reshape(bs, ts * nh, w)

        o_fox_p = fox_prompt_t(pr(fq), pr(fkv), pr(logf))
        o_lat_p = mla_prompt_t(pr(q_lat), pr(q_rope), pr(ckv), pr(krope))
        oc_p, imp_p = nsa_cmp_prompt(pr(nq), pr(ncmp), wcmp, bias_c_p, overlap_p, nc_p)
        sel_p = topk_blocks(imp_p.reshape(n_p, nsp_p), posr_p, ns_p)
        o_nsa_p = nsa_main_prompt_t(pr(nq), pr(seg("ngate", LANE)), oc_p, sel_p, pr(nslc), pr(nwin), btiles)

        o_fox_s = fox_sample(l, page_table, smh(fq, FOX_HD), sm(fkv), sm(logf), fox_pool, logf_pool)
        o_lat_s = mla_sample(l, page_table, smh(q_lat, kv_lora), smh(q_rope, QK_ROPE), sm(ckv), sm(krope),
                             cache_mla_ckv, kr_pool)
        reps = PAGE_SIZE // CMP_D
        wcmp_t = jnp.stack([jnp.tile(wcmp[:CMP_D].T, (1, reps)), jnp.tile(wcmp[CMP_D:].T, (1, reps))])
        oc_s, imp_s = nsa_cmp_sample(l, page_table, smh(nq, NSA_HD), wcmp_t, bias_c_s, overlap_s, cmp_pool,
                                     ts, nc_s)
        sel_s = topk_blocks(imp_s.reshape(n_s, nsp_s), posr_s, ns_s).T.reshape(bs, ts, nsp_s)
        win_all = jnp.concatenate([state_nsa_win_kv[l].reshape(bs, w_buf, -1), sm(nwin)], axis=1)
        o_nsa_s = nsa_main_sample(l, page_table, smh(nq, NSA_HD), smh(seg("ngate"), N_BRANCH), oc_s, sel_s,
                                  sm(nslc), win_t, sm(nwin), b_last, b_new, b_win, b_far, slc_pool, ts)

        o_fox = jnp.concatenate([o_fox_p.reshape(n_p, -1), o_fox_s.reshape(n_s, -1)], axis=0)
        o_lat = jnp.concatenate([o_lat_p.reshape(n_p, -1), o_lat_s.reshape(n_s, -1)], axis=0)
        o_nsa = jnp.concatenate([o_nsa_p.reshape(n_p, -1), o_nsa_s.reshape(n_s, -1)], axis=0)
        o_mla = head_matmul(o_lat, w_uv, BF16)
        mix = jnp.concatenate([o_fox, o_mla, o_nsa], axis=1)
        x = matmul(mix, w_out[l].astype(BF16), res=x)

        h2 = rmsnorm_rows(x, g_ffn[l], BF16)
        gg = matmul(h2, w_gate[l].astype(BF16))
        uu = matmul(h2, w_up[l].astype(BF16))
        gg_p, gg_s = gg[:n_p].reshape(b, t, f_ff), gg[n_p:].reshape(bs, ts, f_ff)
        act_p = conv_act_prompt(gg_p, uu[:n_p].reshape(b, t, f_ff), w_conv[l], b_conv[l])
        ext_s = jnp.concatenate([state_conv[l], gg_s], axis=1)
        act_s = conv_act_rows(*(ext_s[:, k:k + ts].reshape(n_s, f_ff) for k in range(CONV_W)),
                              uu[n_p:], w_conv[l], b_conv[l])
        act = jnp.concatenate([act_p.reshape(n_p, f_ff), act_s], axis=0)
        x = matmul(act, w_down[l].astype(BF16), res=x)

        ext_p = jnp.concatenate([jnp.zeros((b, CONV_W - 1, f_ff), F32), gg_p], axis=1)
        win_p = jnp.pad(pr(nwin), ((0, 0), (w_buf, 0), (0, 0)))[:, -w_buf:]
        kv4 = lambda a, n: a.reshape(a.shape[0], n, 2, -1)
        states.append((
            kv4(pr(fkv), t), kv4(sm(fkv), ts), pr(logf), sm(logf), pr(ckv), sm(ckv), pr(krope), sm(krope),
            kv4(pr(ncmp), t), kv4(sm(ncmp), ts), kv4(pr(nslc), t), kv4(sm(nslc), ts),
            kv4(win_p, w_buf), kv4(win_all[:, -w_buf:], w_buf),
            ext_p[:, -(CONV_W - 1):], ext_s[:, -(CONV_W - 1):]))

    y = rmsnorm_rows(x, g_final, F32)
    stacked = [jnp.stack(z) for z in zip(*states)]
    return (y[:n_p].reshape(b, t, d), y[n_p:].reshape(bs, ts, d)) + tuple(stacked)
```

```python
import functools
import math

import numpy as np
import jax
import jax.numpy as jnp
from jax import lax
from jax.experimental import pallas as pl
from jax.experimental.pallas import tpu as pltpu

F32 = jnp.float32
BF16 = jnp.bfloat16

PAGE_SIZE = 128
FOX_HD = 64
MLA_VHD = 128
QK_NOPE = 128
QK_ROPE = 32
ROPE_THETA = 10000.0
NSA_HD = 64
CMP_L = 32
CMP_D = 16
SLC_B = 64
N_SEL = 16
WINDOW = 512
N_BRANCH = 3
NUM_BUCKETS = 32
MAX_DIST = 128
CONV_W = 3
EPS = 1e-6
NEG = -1e30
BIG = 1e30

LANE = 128
VMEM_LIMIT = 56 * 1024 * 1024
TQ = 128


def _cparams(sem):
    return pltpu.CompilerParams(dimension_semantics=sem, vmem_limit_bytes=VMEM_LIMIT)


def _row_tile(n, cap):
    best = None
    for t in range(8, min(n, cap) + 1, 8):
        if n % t == 0:
            best = t
    return n if best is None else best


def _split3(x):
    a = x.astype(BF16)
    r = x - a.astype(F32)
    b = r.astype(BF16)
    c = (r - b.astype(F32)).astype(BF16)
    return a, b, c


def _dot(a, b):
    return jnp.dot(a, b, preferred_element_type=F32)


def _dot_nt(a, b):
    return lax.dot_general(a, b, (((1,), (1,)), ((), ())), preferred_element_type=F32)


def _sel_dot(sel, x):
    s = sel.astype(BF16)
    a, b, c = _split3(x)
    return _dot(s, a) + _dot(s, b) + _dot(s, c)


def _dot_sel(x, sel):
    s = sel.astype(BF16)
    a, b, c = _split3(x)
    return _dot(a, s) + _dot(b, s) + _dot(c, s)


def _iota(shape, dim):
    return lax.broadcasted_iota(jnp.int32, shape, dim)


def _rms_kernel(x_ref, g_ref, o_ref):
    x = x_ref[...].astype(F32)
    y = x * lax.rsqrt(jnp.mean(x * x, axis=-1, keepdims=True) + EPS)
    o_ref[...] = (y * g_ref[...].astype(F32)).astype(o_ref.dtype)


def rmsnorm_rows(x, g, out_dtype):
    n, d = x.shape
    tm = _row_tile(n, 512)
    return pl.pallas_call(
        _rms_kernel,
        grid=(n // tm,),
        in_specs=[pl.BlockSpec((tm, d), lambda i: (i, 0)), pl.BlockSpec((1, d), lambda i: (0, 0))],
        out_specs=pl.BlockSpec((tm, d), lambda i: (i, 0)),
        out_shape=jax.ShapeDtypeStruct((n, d), out_dtype),
        compiler_params=_cparams(("parallel",)),
        name="rmsnorm",
    )(x, g.reshape(1, d))


def _mm_kernel(*refs, nk, has_res):
    if has_res:
        a_ref, w_ref, r_ref, o_ref = refs[:4]
        scratch = refs[4:]
    else:
        a_ref, w_ref, o_ref = refs[:3]
        r_ref = None
        scratch = refs[3:]
    part = _dot(a_ref[...].astype(BF16), w_ref[...])
    if nk == 1:
        if has_res:
            part = part + r_ref[...]
        o_ref[...] = part.astype(o_ref.dtype)
        return
    acc_ref, = scratch
    k = pl.program_id(2)

    @pl.when(k == 0)
    def _():
        acc_ref[...] = part

    @pl.when(k > 0)
    def _():
        acc_ref[...] += part

    @pl.when(k == nk - 1)
    def _():
        out = acc_ref[...]
        if has_res:
            out = out + r_ref[...]
        o_ref[...] = out.astype(o_ref.dtype)


def matmul(a, w, res=None, out_dtype=F32, tm_cap=1088, tn_cap=512, tk_cap=4096):
    n, kdim = a.shape
    m = w.shape[1]
    tm = _row_tile(n, tm_cap)
    tn = max(t for t in range(LANE, min(m, tn_cap) + 1, LANE) if m % t == 0)
    tk = kdim if kdim <= tk_cap else max(t for t in range(LANE, tk_cap + 1, LANE) if kdim % t == 0)
    nk = kdim // tk
    in_specs = [pl.BlockSpec((tm, tk), lambda i, j, k: (i, k)),
                pl.BlockSpec((tk, tn), lambda i, j, k: (k, j))]
    args = [a, w]
    if res is not None:
        in_specs.append(pl.BlockSpec((tm, tn), lambda i, j, k: (i, j)))
        args.append(res)
    return pl.pallas_call(
        functools.partial(_mm_kernel, nk=nk, has_res=res is not None),
        grid=(n // tm, m // tn, nk),
        in_specs=in_specs,
        out_specs=pl.BlockSpec((tm, tn), lambda i, j, k: (i, j)),
        out_shape=jax.ShapeDtypeStruct((n, m), out_dtype),
        scratch_shapes=[pltpu.VMEM((tm, tn), F32)] if nk > 1 else [],
        compiler_params=_cparams(("parallel", "parallel", "arbitrary")),
        name="matmul",
    )(*args)


def _headmm_kernel(a_ref, w_ref, o_ref):
    o_ref[...] = _dot(a_ref[...].astype(BF16), w_ref[0]).astype(o_ref.dtype)


def head_matmul(a, w, out_dtype):
    n = a.shape[0]
    h, ka, kb = w.shape
    tm = _row_tile(n, 1088)
    return pl.pallas_call(
        _headmm_kernel,
        grid=(n // tm, h),
        in_specs=[pl.BlockSpec((tm, ka), lambda i, j: (i, j)),
                  pl.BlockSpec((1, ka, kb), lambda i, j: (j, 0, 0))],
        out_specs=pl.BlockSpec((tm, kb), lambda i, j: (i, j)),
        out_shape=jax.ShapeDtypeStruct((n, h * kb), out_dtype),
        compiler_params=_cparams(("parallel", "parallel")),
        name="head_matmul",
    )(a, w)


def _pad_to(n, m):
    return -(-n // m) * m


def _in_layout(h, q_lora, kv_lora):
    sizes = [("fq", h * FOX_HD), ("fkv", 2 * FOX_HD), ("ff", h), ("cq", q_lora), ("ckv", kv_lora),
             ("kr", QK_ROPE), ("nq", h * NSA_HD), ("ncmp", 2 * NSA_HD), ("nslc", 2 * NSA_HD),
             ("nwin", 2 * NSA_HD), ("ngate", N_BRANCH * h)]
    off, lay = 0, {}
    for name, sz in sizes:
        lay[name] = (off, sz)
        off += _pad_to(sz, LANE)
    lay["total"] = _pad_to(off, 512)
    return lay


def _pad_w_in(w, lay):
    d = w.shape[0]
    order = ["fq", "fkv", "ff", "cq", "ckv", "kr", "nq", "ncmp", "nslc", "nwin", "ngate"]
    cols, src, pos = [], 0, 0
    for name in order:
        off, sz = lay[name]
        if off > pos:
            cols.append(jnp.zeros((d, off - pos), w.dtype))
        cols.append(w[:, src:src + sz])
        src += sz
        pos = off + sz
    assert src == w.shape[1], (src, w.shape)
    if lay["total"] > pos:
        cols.append(jnp.zeros((d, lay["total"] - pos), w.dtype))
    return jnp.concatenate(cols, axis=1).astype(BF16)


def _rope_lanes(x, cosf, sins, half):
    n = x.shape[-1]
    lane = _iota(x.shape, x.ndim - 1)
    fwd = pltpu.roll(x, n - half, x.ndim - 1)
    bwd = pltpu.roll(x, half, x.ndim - 1)
    partner = jnp.where(lane % (2 * half) < half, fwd, bwd)
    return x * cosf + partner * sins


def _post_kernel(p_ref, bf_ref, gq_ref, gkv_ref, cos_ref, sin_ref,
                 logf_ref, cqn_ref, ckv_ref, kr_ref, *, lay):
    o, s = lay["ff"]
    ff = p_ref[:, o:o + LANE]
    z = ff + bf_ref[...]
    logf = jnp.minimum(z, 0.0) - jnp.log1p(jnp.exp(-jnp.abs(z)))
    logf_ref[...] = logf[:, :s]
    o, s = lay["cq"]
    x = p_ref[:, o:o + s]
    y = x * lax.rsqrt(jnp.mean(x * x, axis=-1, keepdims=True) + EPS)
    cqn_ref[...] = (y * gq_ref[...]).astype(cqn_ref.dtype)
    o, s = lay["ckv"]
    x = p_ref[:, o:o + s]
    y = x * lax.rsqrt(jnp.mean(x * x, axis=-1, keepdims=True) + EPS)
    ckv_ref[...] = y * gkv_ref[...]
    o, s = lay["kr"]
    x = p_ref[:, o:o + LANE]
    kr_ref[...] = _rope_lanes(x, cos_ref[...], sin_ref[...], QK_ROPE // 2)[:, :s]


def post_proj(p, b_fgate, g_qa, g_kva, cos_k, sin_k, lay):
    n, wtot = p.shape
    h = lay["ff"][1]
    q_lora, kv_lora = lay["cq"][1], lay["ckv"][1]
    tm = _row_tile(n, 512)
    bf = jnp.zeros((1, LANE), F32).at[0, :h].set(b_fgate.astype(F32))
    row = lambda w: pl.BlockSpec((tm, w), lambda i: (i, 0))
    full = lambda w: pl.BlockSpec((1, w), lambda i: (0, 0))
    return pl.pallas_call(
        functools.partial(_post_kernel, lay=lay),
        grid=(n // tm,),
        in_specs=[row(wtot), full(LANE), full(q_lora), full(kv_lora), row(LANE), row(LANE)],
        out_specs=[row(h), row(q_lora), row(kv_lora), row(QK_ROPE)],
        out_shape=[jax.ShapeDtypeStruct((n, h), F32), jax.ShapeDtypeStruct((n, q_lora), BF16),
                   jax.ShapeDtypeStruct((n, kv_lora), F32), jax.ShapeDtypeStruct((n, QK_ROPE), F32)],
        compiler_params=_cparams(("parallel",)),
        name="post_proj",
    )(p, bf, g_qa.reshape(1, -1).astype(F32), g_kva.reshape(1, -1).astype(F32), cos_k, sin_k)


def _qrope_kernel(x_ref, cos_ref, sin_ref, o_ref):
    o_ref[...] = _rope_lanes(x_ref[...], cos_ref[...], sin_ref[...], QK_ROPE // 2)


def q_rope_rows(q_mla, nope_w, cos_q, sin_q):
    n = q_mla.shape[0]
    w = q_mla.shape[1] - nope_w
    assert nope_w % w == 0
    tm = _row_tile(n, 1088)
    return pl.pallas_call(
        _qrope_kernel,
        grid=(n // tm,),
        in_specs=[pl.BlockSpec((tm, w), lambda i: (i, nope_w // w)),
                  pl.BlockSpec((tm, w), lambda i: (i, 0)), pl.BlockSpec((tm, w), lambda i: (i, 0))],
        out_specs=pl.BlockSpec((tm, w), lambda i: (i, 0)),
        out_shape=jax.ShapeDtypeStruct((n, w), F32),
        compiler_params=_cparams(("parallel",)),
        name="q_rope",
    )(q_mla, cos_q, sin_q)


def _rope_tables(pos, groups):
    half = QK_ROPE // 2
    freq = ROPE_THETA ** (-jnp.arange(half, dtype=F32) / half)
    ang = pos.astype(F32)[:, None] * freq
    cos, sin = jnp.cos(ang), jnp.sin(ang)
    cosf = jnp.tile(jnp.concatenate([cos, cos], axis=1), (1, groups))
    sins = jnp.tile(jnp.concatenate([-sin, sin], axis=1), (1, groups))
    return cosf, sins


def _flash_init(m_ref, l_ref, acc_ref):
    m_ref[...] = jnp.full(m_ref.shape, NEG, F32)
    l_ref[...] = jnp.zeros(l_ref.shape, F32)
    acc_ref[...] = jnp.zeros(acc_ref.shape, F32)


def _flash_update(s, v, m_ref, l_ref, acc_ref, h, mask=None, v_transposed=False):
    if mask is not None:
        s = jnp.where(mask, s, NEG)
    m_old = m_ref[h]
    m_new = jnp.maximum(m_old, jnp.max(s, axis=-1, keepdims=True))
    alpha = jnp.exp(m_old - m_new)
    p = jnp.exp(s - m_new)
    l_ref[h] = alpha * l_ref[h] + jnp.sum(p, axis=-1, keepdims=True)
    p = p.astype(BF16)
    pv = _dot_nt(p, v) if v_transposed else _dot(p, v)
    acc_ref[h] = alpha * acc_ref[h] + pv
    m_ref[h] = m_new


def _flash_out(l_ref, acc_ref, h):
    l = l_ref[h]
    return acc_ref[h] * (1.0 / jnp.where(l == 0.0, 1.0, l))


def _flash_merge(m_ref, l_ref, acc_ref, dst, src):
    m = jnp.maximum(m_ref[dst], m_ref[src])
    a, b = jnp.exp(m_ref[dst] - m), jnp.exp(m_ref[src] - m)
    l_ref[dst] = a * l_ref[dst] + b * l_ref[src]
    acc_ref[dst] = a * acc_ref[dst] + b * acc_ref[src]
    m_ref[dst] = m


def _place(rows, cols, offset):
    return (_iota((rows, cols), 1) == _iota((rows, cols), 0) + offset).astype(BF16)


def _head_block(q, h, hd):
    per = LANE // hd
    blk = q[:, (h // per) * LANE:(h // per + 1) * LANE]
    sh = (h % per) * hd
    if sh:
        blk = pltpu.roll(blk, LANE - sh, 1)
    return jnp.where(_iota(blk.shape, 1) < hd, blk, 0.0)


def _fox_prompt_kernel(q_ref, kv_ref, logf_ref, o_ref, kcat, cum, qs, m_ref, l_ref, acc_ref,
                       *, nh, t, tk):
    qi = pl.program_id(1)
    tq = TQ
    scale = FOX_HD ** -0.5

    @pl.when(qi == 0)
    def _build():
        kcat[:, :LANE] = kv_ref[0].astype(BF16)
        low = (_iota((tk, tk), 1) <= _iota((tk, tk), 0)).astype(BF16)
        places = [_place(nh, LANE, j * nh) for j in range(3)]

        def body(i, carry):
            r0 = pl.multiple_of(i * tk, tk)
            c = _sel_dot(low, logf_ref[0, pl.ds(r0, tk), :]) + carry
            cum[pl.ds(r0, tk), :] = c
            parts = _split3(-c)
            kcat[pl.ds(r0, tk), LANE:] = sum(_dot(p, pm) for p, pm in zip(parts, places)).astype(BF16)
            return c[tk - 1:tk, :]

        lax.fori_loop(0, t // tk, body, jnp.zeros((1, nh), F32))

    q = q_ref[0]
    lane = _iota((tq, LANE), 1)
    for h in range(nh):
        qh = _head_block(q, h, FOX_HD) * scale
        eh = jnp.where((lane < 3 * nh) & (lane % nh == h), 1.0, 0.0)
        qs[h] = jnp.concatenate([qh, eh], axis=1).astype(BF16)
    _flash_init(m_ref, l_ref, acc_ref)
    cq = cum[pl.ds(pl.multiple_of(qi * tq, tq), tq), :]
    qpos = qi * tq + _iota((tq, tk), 0)

    def step(kt, masked):
        r0 = pl.multiple_of(kt * tk, tk)
        kc = kcat[pl.ds(r0, tk), :]
        v = kc[:, :LANE]
        mask = (r0 + _iota((tq, tk), 1) <= qpos) if masked else None
        for h in range(nh):
            s = _dot_nt(qs[h], kc) + cq[:, h:h + 1]
            _flash_update(s, v, m_ref, l_ref, acc_ref, h, mask)

    n_kt = ((qi + 1) * tq + tk - 1) // tk

    def loop_body(kt, carry):
        step(kt, False)
        return carry

    lax.fori_loop(0, n_kt - 1, loop_body, 0)
    step(n_kt - 1, True)
    for h in range(nh):
        o_ref[0, :, h * FOX_HD:(h + 1) * FOX_HD] = _flash_out(l_ref, acc_ref, h)[:, FOX_HD:].astype(o_ref.dtype)


def fox_prompt(fq, fkv, logf):
    b, t, _ = fq.shape
    nh = logf.shape[-1]
    tk = min(256, t)
    return pl.pallas_call(
        functools.partial(_fox_prompt_kernel, nh=nh, t=t, tk=tk),
        grid=(b, t // TQ),
        in_specs=[pl.BlockSpec((1, TQ, nh * FOX_HD), lambda i, j: (i, j, 0)),
                  pl.BlockSpec((1, t, 2 * FOX_HD), lambda i, j: (i, 0, 0)),
                  pl.BlockSpec((1, t, nh), lambda i, j: (i, 0, 0))],
        out_specs=pl.BlockSpec((1, TQ, nh * FOX_HD), lambda i, j: (i, j, 0)),
        out_shape=jax.ShapeDtypeStruct((b, t, nh * FOX_HD), BF16),
        scratch_shapes=[pltpu.VMEM((t, 2 * LANE), BF16), pltpu.VMEM((t, nh), F32),
                        pltpu.VMEM((nh, TQ, 2 * LANE), BF16), pltpu.VMEM((nh, TQ, 1), F32),
                        pltpu.VMEM((nh, TQ, 1), F32), pltpu.VMEM((nh, TQ, LANE), F32)],
        compiler_params=_cparams(("arbitrary", "arbitrary")),
        name="fox_prompt",
    )(fq, fkv, logf)


def _mla_prompt_kernel(ql_ref, qr_ref, ckv_ref, kr_ref, o_ref, kcat, qs, m_ref, l_ref, acc_ref,
                       *, nh, c, tk):
    qi = pl.program_id(1)
    tq = TQ
    scale = (QK_NOPE + QK_ROPE) ** -0.5

    @pl.when(qi == 0)
    def _build():
        kcat[:, :c] = ckv_ref[0].astype(BF16)
        kcat[:, c:] = _dot(kr_ref[0].astype(BF16), _place(QK_ROPE, LANE, 0)).astype(BF16)

    qr = (qr_ref[0] * scale).astype(BF16)
    for h in range(nh):
        sel = (_iota((nh * QK_ROPE, LANE), 0) == _iota((nh * QK_ROPE, LANE), 1) + h * QK_ROPE).astype(BF16)
        qs[h, :, :c] = (ql_ref[0, :, h * c:(h + 1) * c] * scale).astype(BF16)
        qs[h, :, c:] = _dot(qr, sel).astype(BF16)
    _flash_init(m_ref, l_ref, acc_ref)
    qpos = qi * tq + _iota((tq, tk), 0)

    def step(kt, masked):
        r0 = pl.multiple_of(kt * tk, tk)
        kc = kcat[pl.ds(r0, tk), :]
        v = kc[:, :c]
        mask = (r0 + _iota((tq, tk), 1) <= qpos) if masked else None
        for h in range(nh):
            _flash_update(_dot_nt(qs[h], kc), v, m_ref, l_ref, acc_ref, h, mask)

    n_kt = ((qi + 1) * tq + tk - 1) // tk

    def loop_body(kt, carry):
        step(kt, False)
        return carry

    lax.fori_loop(0, n_kt - 1, loop_body, 0)
    step(n_kt - 1, True)
    for h in range(nh):
        o_ref[0, :, h * c:(h + 1) * c] = _flash_out(l_ref, acc_ref, h).astype(o_ref.dtype)


def mla_prompt(q_lat, q_rope, ckv, krope):
    b, t, c = ckv.shape
    nh = q_lat.shape[-1] // c
    tk = min(256, t)
    return pl.pallas_call(
        functools.partial(_mla_prompt_kernel, nh=nh, c=c, tk=tk),
        grid=(b, t // TQ),
        in_specs=[pl.BlockSpec((1, TQ, nh * c), lambda i, j: (i, j, 0)),
                  pl.BlockSpec((1, TQ, nh * QK_ROPE), lambda i, j: (i, j, 0)),
                  pl.BlockSpec((1, t, c), lambda i, j: (i, 0, 0)),
                  pl.BlockSpec((1, t, QK_ROPE), lambda i, j: (i, 0, 0))],
        out_specs=pl.BlockSpec((1, TQ, nh * c), lambda i, j: (i, j, 0)),
        out_shape=jax.ShapeDtypeStruct((b, t, nh * c), BF16),
        scratch_shapes=[pltpu.VMEM((t, c + LANE), BF16), pltpu.VMEM((nh, TQ, c + LANE), BF16),
                        pltpu.VMEM((nh, TQ, 1), F32), pltpu.VMEM((nh, TQ, 1), F32),
                        pltpu.VMEM((nh, TQ, c), F32)],
        compiler_params=_cparams(("arbitrary", "arbitrary")),
        name="mla_prompt",
    )(q_lat, q_rope, ckv, krope)


TILE_PAGES = 32
NEW_PAD = 16


def _pages_dma(pool_ref, layer, pt_ref, b, page0, npages, buf, slot, sem, start, prow=PAGE_SIZE):
    def body(j, c):
        pg = pt_ref[b, page0 + j]
        cp = pltpu.make_async_copy(
            pool_ref.at[layer, pg],
            buf.at[slot, pl.ds(pl.multiple_of(j * prow, prow), prow)],
            sem.at[slot])
        if start:
            cp.start()
        else:
            cp.wait()
        return c
    lax.fori_loop(0, npages, body, 0)


def _paged_pipeline(g, n_steps, issue):
    slot = g % 2

    @pl.when(g == 0)
    def _():
        issue(g, slot, True)

    @pl.when(g + 1 < n_steps)
    def _():
        issue(g + 1, 1 - slot, True)

    issue(g, slot, False)
    return slot


def _pad_rows(x, rows):
    if x.shape[0] == rows:
        return x
    return jnp.concatenate([x, jnp.zeros((rows - x.shape[0],) + x.shape[1:], x.dtype)], axis=0)


def _lane_cat(ref, slot, starts, rows):
    return jnp.concatenate([ref[slot, pl.ds(s, rows), :] for s in starts], axis=1)


def _fox_sample_kernel(pt_ref, q_ref, kvn_ref, lfn_ref, kvpool, lfpool, o_ref,
                       kvbuf, lfbuf, sems, qaug, cnq, knew, suf_s, carry_s, m_ref, l_ref, acc_ref,
                       *, layer, nh, ts, nchunk, cpages, ptile, npar, n_steps):
    g = pl.program_id(0)
    c = g % nchunk
    r = ts * nh
    hp = _pad_to(nh, 16)
    kq = FOX_HD + 3 * hp
    scale = FOX_HD ** -0.5

    def issue(step, slot, start):
        sb = step // nchunk
        p0 = (nchunk - 1 - step % nchunk) * cpages
        _pages_dma(kvpool, layer, pt_ref, sb, p0, cpages, kvbuf, slot, sems.at[0], start)
        _pages_dma(lfpool, layer, pt_ref, sb, p0, cpages, lfbuf, slot, sems.at[1], start, nh)

    slot = _paged_pipeline(g, n_steps, issue)

    @pl.when(c == 0)
    def _prep():
        row = _iota((r, kq), 0)
        lane = _iota((r, kq), 1)
        qp = _dot((q_ref[0] * scale).astype(BF16), _place(FOX_HD, kq, 0))
        eh = jnp.where((lane >= FOX_HD) & ((lane - FOX_HD) % hp == row % nh), 1.0, 0.0)
        qaug[...] = (qp + eh).astype(BF16)
        low = (_iota((NEW_PAD, NEW_PAD), 1) <= _iota((NEW_PAD, NEW_PAD), 0)).astype(BF16)
        cn = _sel_dot(low, lfn_ref[0])
        pick = (_iota((r, NEW_PAD), 1) == _iota((r, NEW_PAD), 0) // nh).astype(BF16)
        gq = _sel_dot(pick, cn)
        hsel = _iota((r, nh), 1) == _iota((r, nh), 0) % nh
        cnq[...] = jnp.sum(jnp.where(hsel, gq, 0.0), axis=1, keepdims=True)
        kn = _dot(kvn_ref[0][:, :FOX_HD].astype(BF16), _place(FOX_HD, kq, 0))
        for i, part in enumerate(_split3(-cn)):
            kn = kn + _dot(part, _place(nh, kq, FOX_HD + hp * i))
        knew[...] = kn.astype(BF16)
        carry_s[...] = jnp.zeros(carry_s.shape, F32)
        _flash_init(m_ref, l_ref, acc_ref)

    x_all = lfbuf[slot]
    upper = (_iota((PAGE_SIZE, PAGE_SIZE), 0) > _iota((PAGE_SIZE, PAGE_SIZE), 1)).astype(BF16)
    suf_s[...] = _dot_sel(x_all, upper)
    ntile = cpages // ptile

    def tile(i, carry):
        for u in range(npar):
            ti = ntile - 1 - (i * npar + u)
            sp = [None] * ptile
            for jj in reversed(range(ptile)):
                row0 = pl.multiple_of((ti * ptile + jj) * nh, nh)
                sl = suf_s[pl.ds(row0, nh), :]
                sp[jj] = sl + carry
                carry = carry + sl[:, 0:1] + lfbuf[slot, pl.ds(row0, nh), 0:1]
            parts = [_pad_rows(p, hp) for p in _split3(jnp.concatenate(sp, axis=1))]
            starts = [pl.multiple_of((ti * ptile + jj) * PAGE_SIZE, PAGE_SIZE) for jj in range(ptile)]
            k_t = _lane_cat(kvbuf, slot, starts, FOX_HD).astype(BF16)
            v_t = _lane_cat(kvbuf, slot, [s + FOX_HD for s in starts], FOX_HD).astype(BF16)
            s = _dot(qaug[...], jnp.concatenate([k_t] + parts, axis=0)) + cnq[...]
            _flash_update(s, v_t, m_ref, l_ref, acc_ref, u, v_transposed=True)
        return carry

    carry_s[...] = lax.fori_loop(0, ntile // npar, tile, carry_s[...])

    @pl.when(c == nchunk - 1)
    def _fin():
        for u in range(1, npar):
            _flash_merge(m_ref, l_ref, acc_ref, 0, u)
        s = _dot_nt(qaug[...], knew[...]) + cnq[...]
        qi = _iota((r, NEW_PAD), 0) // nh
        kj = _iota((r, NEW_PAD), 1)
        vn = kvn_ref[0][:, FOX_HD:].astype(BF16)
        _flash_update(s, vn, m_ref, l_ref, acc_ref, 0, (kj <= qi) & (kj < ts))
        o_ref[0] = _flash_out(l_ref, acc_ref, 0).astype(o_ref.dtype)


def _paged_call(kern, n_steps, in_blocks, pools, out_block, out_shape, scratch, name):
    grid_spec = pltpu.PrefetchScalarGridSpec(
        num_scalar_prefetch=1,
        grid=(n_steps,),
        in_specs=in_blocks + [pl.BlockSpec(memory_space=pl.ANY)] * pools,
        out_specs=out_block,
        scratch_shapes=scratch)
    return pl.pallas_call(kern, grid_spec=grid_spec, out_shape=out_shape,
                          compiler_params=_cparams(("arbitrary",)), name=name)


def _tile_plan(cpages, max_par):
    ptile = max(p for p in (TILE_PAGES, 8, 4, 2, 1) if cpages % p == 0)
    npar = max(n for n in (4, 2, 1) if n <= max_par and (cpages // ptile) % n == 0)
    return ptile, npar


def _pad_new(x):
    return jnp.pad(x, ((0, 0), (0, NEW_PAD - x.shape[1]), (0, 0)))


def fox_sample(layer, page_table, fq, fkv_new, logf_new, kv_pool, logf_pool):
    bs, r, _ = fq.shape
    ts = fkv_new.shape[1]
    nh = r // ts
    npages = page_table.shape[1]
    nchunk = 1
    cpages = npages // nchunk
    ptile, npar = _tile_plan(cpages, 4)
    n_steps = bs * nchunk
    kq = FOX_HD + 3 * _pad_to(nh, 16)
    per_b = lambda w: pl.BlockSpec((1,) + w, lambda g, pt: (g // nchunk, 0, 0))
    kern = functools.partial(_fox_sample_kernel, layer=layer, nh=nh, ts=ts, nchunk=nchunk,
                             cpages=cpages, ptile=ptile, npar=npar, n_steps=n_steps)
    scratch = [pltpu.VMEM((2, cpages * PAGE_SIZE, LANE), F32), pltpu.VMEM((2, cpages * nh, LANE), F32),
               pltpu.SemaphoreType.DMA((2, 2)),
               pltpu.VMEM((r, kq), BF16), pltpu.VMEM((r, 1), F32), pltpu.VMEM((NEW_PAD, kq), BF16),
               pltpu.VMEM((cpages * nh, LANE), F32), pltpu.VMEM((nh, 1), F32),
               pltpu.VMEM((npar, r, 1), F32), pltpu.VMEM((npar, r, 1), F32), pltpu.VMEM((npar, r, FOX_HD), F32)]
    return _paged_call(
        kern, n_steps, [per_b((r, FOX_HD)), per_b((NEW_PAD, LANE)), per_b((NEW_PAD, nh))], 2,
        per_b((r, FOX_HD)), jax.ShapeDtypeStruct((bs, r, FOX_HD), BF16), scratch, "fox_sample",
    )(page_table, fq, _pad_new(fkv_new), _pad_new(logf_new), kv_pool, logf_pool)


def _mla_sample_kernel(pt_ref, ql_ref, qr_ref, ckvn_ref, krn_ref, ckvpool, krpool, o_ref,
                       ckvbuf, krbuf, sems, qlat, qrope, m_ref, l_ref, acc_ref,
                       *, layer, nh, ts, nchunk, cpages, ptile, npar, n_steps):
    g = pl.program_id(0)
    c = g % nchunk
    r = ts * nh
    scale = (QK_NOPE + QK_ROPE) ** -0.5

    def issue(step, slot, start):
        sb = step // nchunk
        p0 = (step % nchunk) * cpages
        _pages_dma(ckvpool, layer, pt_ref, sb, p0, cpages, ckvbuf, slot, sems.at[0], start)
        _pages_dma(krpool, layer, pt_ref, sb, p0, cpages, krbuf, slot, sems.at[1], start, QK_ROPE)

    slot = _paged_pipeline(g, n_steps, issue)

    @pl.when(c == 0)
    def _prep():
        qlat[...] = (ql_ref[0] * scale).astype(BF16)
        qrope[...] = (qr_ref[0] * scale).astype(BF16)
        _flash_init(m_ref, l_ref, acc_ref)

    def tile(i, cy):
        for u in range(npar):
            ti = i * npar + u
            r0 = pl.multiple_of(ti * ptile * PAGE_SIZE, ptile * PAGE_SIZE)
            ckv = ckvbuf[slot, pl.ds(r0, ptile * PAGE_SIZE), :].astype(BF16)
            starts = [pl.multiple_of((ti * ptile + jj) * QK_ROPE, QK_ROPE) for jj in range(ptile)]
            kr_t = _lane_cat(krbuf, slot, starts, QK_ROPE).astype(BF16)
            s = _dot_nt(qlat[...], ckv) + _dot(qrope[...], kr_t)
            _flash_update(s, ckv, m_ref, l_ref, acc_ref, u)
        return cy

    lax.fori_loop(0, cpages // (ptile * npar), tile, 0)

    @pl.when(c == nchunk - 1)
    def _fin():
        for u in range(1, npar):
            _flash_merge(m_ref, l_ref, acc_ref, 0, u)
        ckv = ckvn_ref[0].astype(BF16)
        s = _dot_nt(qlat[...], ckv) + _dot_nt(qrope[...], krn_ref[0].astype(BF16))
        qi = _iota((r, NEW_PAD), 0) // nh
        kj = _iota((r, NEW_PAD), 1)
        _flash_update(s, ckv, m_ref, l_ref, acc_ref, 0, (kj <= qi) & (kj < ts))
        o_ref[0] = _flash_out(l_ref, acc_ref, 0).astype(o_ref.dtype)


def mla_sample(layer, page_table, q_lat, q_rope, ckv_new, kr_new, ckv_pool, kr_pool):
    bs, r, c = q_lat.shape
    ts = ckv_new.shape[1]
    nh = r // ts
    npages = page_table.shape[1]
    nchunk = 2 if npages % 2 == 0 and npages >= 16 else 1
    cpages = npages // nchunk
    ptile, npar = _tile_plan(cpages, 2)
    n_steps = bs * nchunk
    per_b = lambda w: pl.BlockSpec((1,) + w, lambda g, pt: (g // nchunk, 0, 0))
    kern = functools.partial(_mla_sample_kernel, layer=layer, nh=nh, ts=ts, nchunk=nchunk,
                             cpages=cpages, ptile=ptile, npar=npar, n_steps=n_steps)
    scratch = [pltpu.VMEM((2, cpages * PAGE_SIZE, c), F32), pltpu.VMEM((2, cpages * QK_ROPE, LANE), F32),
               pltpu.SemaphoreType.DMA((2, 2)),
               pltpu.VMEM((r, c), BF16), pltpu.VMEM((r, QK_ROPE), BF16),
               pltpu.VMEM((npar, r, 1), F32), pltpu.VMEM((npar, r, 1), F32), pltpu.VMEM((npar, r, c), F32)]
    return _paged_call(
        kern, n_steps,
        [per_b((r, c)), per_b((r, QK_ROPE)), per_b((NEW_PAD, c)), per_b((NEW_PAD, QK_ROPE))], 2,
        per_b((r, c)), jax.ShapeDtypeStruct((bs, r, c), BF16), scratch, "mla_sample",
    )(page_table, q_lat, q_rope, _pad_new(ckv_new), _pad_new(kr_new), ckv_pool, kr_pool)


def _bucket_np(dist):
    n = np.maximum(dist, 0)
    max_exact = NUM_BUCKETS // 2
    large = max_exact + (np.log(np.maximum(n, 1).astype(np.float32) / max_exact)
                         / math.log(MAX_DIST / max_exact) * (NUM_BUCKETS - max_exact)).astype(np.int32)
    return np.where(n < max_exact, n, np.minimum(large, NUM_BUCKETS - 1)).astype(np.int32)


def _bias_kernel(tab_ref, bk_ref, o_ref):
    h = pl.program_id(0)
    bk = bk_ref[...]
    acc = jnp.zeros(bk.shape, F32)
    for b in range(NUM_BUCKETS):
        acc = jnp.where(bk == b, tab_ref[b, h], acc)
    o_ref[0] = acc


def bias_lookup(table, bucket):
    nh = table.shape[1]
    m, n = bucket.shape
    tm = _row_tile(m, 256) if m % 8 == 0 else m
    return pl.pallas_call(
        _bias_kernel,
        grid=(nh, m // tm),
        in_specs=[pl.BlockSpec(memory_space=pltpu.SMEM), pl.BlockSpec((tm, n), lambda i, j: (j, 0))],
        out_specs=pl.BlockSpec((1, tm, n), lambda i, j: (i, j, 0)),
        out_shape=jax.ShapeDtypeStruct((nh, m, n), F32),
        compiler_params=_cparams(("parallel", "parallel")),
        name="bias_lookup",
    )(table.astype(F32), jnp.asarray(bucket))


def _overlap_np(ncp, nsp, nc, ns):
    i = np.arange(ncp)[:, None] * CMP_D
    j = np.arange(nsp)[None, :] * SLC_B
    ov = np.maximum(np.minimum(i + CMP_L, j + SLC_B) - np.maximum(i, j), 0) / CMP_L
    ov = np.where((np.arange(ncp)[:, None] < nc) & (np.arange(nsp)[None, :] < ns), ov, 0.0)
    return ov.astype(np.float32)


def _pool_page(x, wcmp, h1, h2, row0):
    per = PAGE_SIZE // CMP_D
    g = (_iota((per, PAGE_SIZE), 1) // CMP_D == _iota((per, PAGE_SIZE), 0)).astype(BF16)
    wa = jnp.concatenate([wcmp[0:CMP_D]] * per, axis=0)
    wb = jnp.concatenate([wcmp[CMP_D:CMP_L]] * per, axis=0)
    h1[pl.ds(row0, per), :] = _sel_dot(g, x * wa)
    h2[pl.ds(row0, per), :] = _sel_dot(g, x * wb)


def _cmp_softmax(s, mask):
    s = jnp.where(mask, s, NEG)
    m = jnp.max(s, axis=-1, keepdims=True)
    p = jnp.where(mask, jnp.exp(s - m), 0.0)
    l = jnp.sum(p, axis=-1, keepdims=True)
    return p * (1.0 / jnp.where(l == 0.0, 1.0, l))


def _nsa_cmp_prompt_kernel(q_ref, x_ref, w_ref, bias_ref, ov_ref, oc_ref, imp_ref, kvc, h1, h2,
                           *, nh, t, nc, ncp):
    qi = pl.program_id(1)
    tq = TQ
    scale = NSA_HD ** -0.5
    per = PAGE_SIZE // CMP_D

    @pl.when(qi == 0)
    def _pool():
        h1[...] = jnp.zeros(h1.shape, F32)
        h2[...] = jnp.zeros(h2.shape, F32)
        w = w_ref[...]

        def body(c, cy):
            r0 = pl.multiple_of(c * PAGE_SIZE, PAGE_SIZE)
            _pool_page(x_ref[0, pl.ds(r0, PAGE_SIZE), :], w, h1, h2, pl.multiple_of(c * per, per))
            return cy

        lax.fori_loop(0, t // PAGE_SIZE, body, 0)
        kvc[...] = (h1[pl.ds(0, ncp), :] + h2[pl.ds(1, ncp), :]).astype(BF16)

    q = q_ref[0]
    kc = kvc[...]
    pos = qi * tq + _iota((tq, ncp), 0)
    n = _iota((tq, ncp), 1)
    mask = (n * CMP_D + CMP_L - 1 <= pos) & (n < nc)
    psum = jnp.zeros((tq, ncp), F32)
    for h in range(nh):
        qh = (_head_block(q, h, NSA_HD) * scale).astype(BF16)
        p = _cmp_softmax(_dot_nt(qh, kc) + bias_ref[h], mask)
        oc_ref[0, :, h * LANE:(h + 1) * LANE] = _dot(p.astype(BF16), kc)
        psum = psum + p
    imp_ref[0] = _dot_sel(psum, ov_ref[...])


def nsa_cmp_prompt(nq, ncmp, wcmp, bias_c, overlap, nc):
    b, t, _ = nq.shape
    nh, _, ncp = bias_c.shape
    nsp = overlap.shape[1]
    hb = t // CMP_D
    return pl.pallas_call(
        functools.partial(_nsa_cmp_prompt_kernel, nh=nh, t=t, nc=nc, ncp=ncp),
        grid=(b, t // TQ),
        in_specs=[pl.BlockSpec((1, TQ, nh * NSA_HD), lambda i, j: (i, j, 0)),
                  pl.BlockSpec((1, t, LANE), lambda i, j: (i, 0, 0)),
                  pl.BlockSpec((CMP_L, LANE), lambda i, j: (0, 0)),
                  pl.BlockSpec((nh, TQ, ncp), lambda i, j: (0, j, 0)),
                  pl.BlockSpec((ncp, nsp), lambda i, j: (0, 0))],
        out_specs=[pl.BlockSpec((1, TQ, nh * LANE), lambda i, j: (i, j, 0)),
                   pl.BlockSpec((1, TQ, nsp), lambda i, j: (i, j, 0))],
        out_shape=[jax.ShapeDtypeStruct((b, t, nh * LANE), F32), jax.ShapeDtypeStruct((b, t, nsp), F32)],
        scratch_shapes=[pltpu.VMEM((ncp, LANE), BF16), pltpu.VMEM((max(hb, ncp) + 8, LANE), F32),
                        pltpu.VMEM((max(hb, ncp) + 8, LANE), F32)],
        compiler_params=_cparams(("arbitrary", "arbitrary")),
        name="nsa_cmp_prompt",
    )(nq, ncmp, wcmp, bias_c, overlap)


def _nsa_cmp_sample_kernel(pt_ref, q_ref, w_ref, bias_ref, ov_ref, pool, oc_ref, imp_ref,
                           buf, sems, kvc, h1, h2, *, layer, nh, ts, npages, nc, ncp, n_steps, past_len):
    g = pl.program_id(0)
    r = ts * nh
    scale = NSA_HD ** -0.5
    per = PAGE_SIZE // CMP_D

    def issue(step, slot, start):
        _pages_dma(pool, layer, pt_ref, step, 0, npages, buf, slot, sems, start)

    slot = _paged_pipeline(g, n_steps, issue)
    h2[...] = jnp.zeros(h2.shape, F32)
    w1, w2 = w_ref[0], w_ref[1]
    gsel = (_iota((per, PAGE_SIZE), 1) // CMP_D == _iota((per, PAGE_SIZE), 0)).astype(BF16)
    unroll = max(u for u in (8, 4, 2, 1) if npages % u == 0)

    def body(c, cy):
        for u in range(unroll):
            pg = c * unroll + u
            xt = buf[slot, pl.ds(pl.multiple_of(pg * PAGE_SIZE, PAGE_SIZE), PAGE_SIZE), :]
            row0 = pl.multiple_of(pg * per, per)
            h1[pl.ds(row0, per), :] = _dot_nt(gsel, (xt * w1).astype(BF16))
            h2[pl.ds(row0, per), :] = _dot_nt(gsel, (xt * w2).astype(BF16))
        return cy

    lax.fori_loop(0, npages // unroll, body, 0)
    kvc[...] = (h1[pl.ds(0, ncp), :] + h2[pl.ds(1, ncp), :]).astype(BF16)
    kc = kvc[...]
    qp = _dot((q_ref[0] * scale).astype(BF16), _place(NSA_HD, LANE, 0)).astype(BF16)
    pos = past_len + _iota((r, ncp), 0) // nh
    n = _iota((r, ncp), 1)
    mask = (n * CMP_D + CMP_L - 1 <= pos) & (n < nc)
    p = _cmp_softmax(_dot_nt(qp, kc) + bias_ref[...], mask)
    oc_ref[0] = _dot(p.astype(BF16), kc)
    rows = (_iota((ts, r), 1) // nh == _iota((ts, r), 0)).astype(BF16)
    imp_ref[0] = _dot_sel(_sel_dot(rows, p), ov_ref[...])


def nsa_cmp_sample(layer, page_table, nq, wcmp, bias_c, overlap, pool, ts, nc):
    bs, r, _ = nq.shape
    nh = r // ts
    npages = page_table.shape[1]
    ncp = bias_c.shape[1]
    nsp = overlap.shape[1]
    hb = npages * (PAGE_SIZE // CMP_D)
    assert ncp == hb
    kern = functools.partial(_nsa_cmp_sample_kernel, layer=layer, nh=nh, ts=ts, npages=npages, nc=nc,
                             ncp=ncp, n_steps=bs, past_len=npages * PAGE_SIZE)
    per_b = lambda w: pl.BlockSpec((1,) + w, lambda g, pt: (g, 0, 0))
    const = lambda s: pl.BlockSpec(s, lambda g, pt: (0, 0))
    grid_spec = pltpu.PrefetchScalarGridSpec(
        num_scalar_prefetch=1, grid=(bs,),
        in_specs=[per_b((r, NSA_HD)), pl.BlockSpec((2, LANE, PAGE_SIZE), lambda g, pt: (0, 0, 0)),
                  const((r, ncp)), const((ncp, nsp)), pl.BlockSpec(memory_space=pl.ANY)],
        out_specs=[per_b((r, LANE)), per_b((ts, nsp))],
        scratch_shapes=[pltpu.VMEM((2, npages * PAGE_SIZE, LANE), F32), pltpu.SemaphoreType.DMA((2,)),
                        pltpu.VMEM((ncp, LANE), BF16), pltpu.VMEM((hb + 8, LANE), F32),
                        pltpu.VMEM((hb + 8, LANE), F32)])
    return pl.pallas_call(
        kern, grid_spec=grid_spec,
        out_shape=[jax.ShapeDtypeStruct((bs, r, LANE), F32), jax.ShapeDtypeStruct((bs, ts, nsp), F32)],
        compiler_params=_cparams(("arbitrary",)), name="nsa_cmp_sample",
    )(page_table, nq, wcmp, bias_c, overlap, pool)


def _topk_kernel(imp_ref, pos_ref, o_ref, score, *, ns, n_sel):
    imp = imp_ref[...]
    pos = pos_ref[...]
    blk = _iota(imp.shape, 0)
    cur = pos // SLC_B
    forced = (blk == 0) | (blk == cur) | (blk == cur - 1)
    sc = jnp.where(forced, BIG, jnp.where(blk * SLC_B <= pos, imp, -BIG))
    sc = jnp.where(blk < ns, sc, -3e38)
    score[...] = sc

    def body(j, rank):
        sj = score[pl.ds(j, 1), :]
        tie = jnp.where(blk > j, 1.0, 0.0)
        return rank + jnp.where(sj > sc, 1.0, jnp.where(sj == sc, tie, 0.0))

    rank = lax.fori_loop(0, ns, body, jnp.zeros(imp.shape, F32))
    o_ref[...] = jnp.where(rank < n_sel, 1.0, 0.0)


def topk_blocks(imp, pos, ns):
    n, nsp = imp.shape
    tn = LANE if n % LANE == 0 else n
    sel_t = pl.pallas_call(
        functools.partial(_topk_kernel, ns=ns, n_sel=min(N_SEL, ns)),
        grid=(n // tn,),
        in_specs=[pl.BlockSpec((nsp, tn), lambda i: (0, i)), pl.BlockSpec((1, tn), lambda i: (0, i))],
        out_specs=pl.BlockSpec((nsp, tn), lambda i: (0, i)),
        out_shape=jax.ShapeDtypeStruct((nsp, n), F32),
        scratch_shapes=[pltpu.VMEM((nsp, tn), F32)],
        compiler_params=_cparams(("parallel",)),
        name="topk_blocks",
    )(imp.T, pos.reshape(1, n).astype(jnp.int32))
    return sel_t


def _nsa_main_prompt_kernel(q_ref, gate_ref, oc_ref, sel_ref, slc_ref, win_ref, bt_ref, o_ref,
                            slcb, winb, qs, selm, m_ref, l_ref, acc_ref, *, nh, t, nsp):
    qi = pl.program_id(1)
    tq = tk = TQ
    scale = NSA_HD ** -0.5

    @pl.when(qi == 0)
    def _cast():
        slcb[...] = slc_ref[0].astype(BF16)
        winb[...] = win_ref[0].astype(BF16)

    q = q_ref[0]
    for h in range(nh):
        qs[h] = (_head_block(q, h, NSA_HD) * scale).astype(BF16)
    expand = (_iota((nsp, t), 1) // SLC_B == _iota((nsp, t), 0)).astype(BF16)
    selm[...] = _dot(sel_ref[0].astype(BF16), expand)
    _flash_init(m_ref, l_ref, acc_ref)
    dq = _iota((tq, tk), 0) - _iota((tq, tk), 1)

    def step(kt, cy):
        r0 = pl.multiple_of(kt * tk, tk)
        dist = (qi - kt) * tk + dq
        b0 = pl.multiple_of(jnp.minimum(qi - kt, 2) * tk, tk)
        kc = slcb[pl.ds(r0, tk), :]
        mask_s = jnp.where(dist >= 0, selm[:, pl.ds(r0, tk)], 0.0) > 0.5
        for h in range(nh):
            s = _dot_nt(qs[h], kc) + bt_ref[h, pl.ds(b0, tk), :]
            _flash_update(s, kc, m_ref, l_ref, acc_ref, h, mask_s)

        @pl.when((qi - kt) * tk < WINDOW + tq)
        def _win():
            kw = winb[pl.ds(r0, tk), :]
            mask_w = jnp.where(dist >= 0, dist, WINDOW) < WINDOW
            for h in range(nh):
                s = _dot_nt(qs[h], kw) + bt_ref[h, pl.ds(b0, tk), :]
                _flash_update(s, kw, m_ref, l_ref, acc_ref, nh + h, mask_w)
        return cy

    lax.fori_loop(0, qi + 1, step, 0)
    g = jax.nn.sigmoid(gate_ref[0])
    for h in range(nh):
        o = (g[:, 3 * h:3 * h + 1] * oc_ref[0, :, h * LANE:(h + 1) * LANE]
             + g[:, 3 * h + 1:3 * h + 2] * _flash_out(l_ref, acc_ref, h)
             + g[:, 3 * h + 2:3 * h + 3] * _flash_out(l_ref, acc_ref, nh + h))
        o_ref[0, :, h * NSA_HD:(h + 1) * NSA_HD] = o[:, NSA_HD:].astype(o_ref.dtype)


def nsa_main_prompt(nq, gates, o_c, sel, nslc, nwin, btiles):
    b, t, _ = nq.shape
    nh = btiles.shape[0]
    nsp = sel.shape[-1]
    qblk = lambda w: pl.BlockSpec((1, TQ, w), lambda i, j: (i, j, 0))
    full = lambda w: pl.BlockSpec((1, t, w), lambda i, j: (i, 0, 0))
    return pl.pallas_call(
        functools.partial(_nsa_main_prompt_kernel, nh=nh, t=t, nsp=nsp),
        grid=(b, t // TQ),
        in_specs=[qblk(nh * NSA_HD), qblk(LANE), qblk(nh * LANE), qblk(nsp), full(LANE), full(LANE),
                  pl.BlockSpec(btiles.shape, lambda i, j: (0, 0, 0))],
        out_specs=qblk(nh * NSA_HD),
        out_shape=jax.ShapeDtypeStruct((b, t, nh * NSA_HD), BF16),
        scratch_shapes=[pltpu.VMEM((t, LANE), BF16), pltpu.VMEM((t, LANE), BF16),
                        pltpu.VMEM((nh, TQ, LANE), BF16), pltpu.VMEM((TQ, t), F32),
                        pltpu.VMEM((2 * nh, TQ, 1), F32), pltpu.VMEM((2 * nh, TQ, 1), F32),
                        pltpu.VMEM((2 * nh, TQ, LANE), F32)],
        compiler_params=_cparams(("arbitrary", "arbitrary")),
        name="nsa_main_prompt",
    )(nq, gates, o_c, sel, nslc, nwin, btiles)


def _nsa_main_sample_kernel(pt_ref, fl_ref, q_ref, gate_ref, oc_ref, sel_ref, slcn_ref, win_ref, winn_ref,
                            blast_ref, bnew_ref, bwin_ref, bfar_ref, pool, o_ref,
                            buf, sems, plist, pcnt, qp, selr, m_ref, l_ref, acc_ref,
                            *, layer, nh, ts, npages, maxp, ptile, n_steps, past_len, wb):
    g = pl.program_id(0)
    r = ts * nh
    scale = NSA_HD ** -0.5
    hd = NSA_HD

    def page_copy(pg, k, slot):
        return pltpu.make_async_copy(
            pool.at[layer, pg], buf.at[slot, pl.ds(pl.multiple_of(k * PAGE_SIZE, PAGE_SIZE), PAGE_SIZE)],
            sems.at[slot])

    def issue(step, slot):
        def body(j, cnt):
            take = jnp.logical_and(fl_ref[step, j] != 0, cnt < maxp)

            @pl.when(take)
            def _():
                page_copy(pt_ref[step, j], cnt, slot).start()
                plist[slot, cnt] = j
            return cnt + take.astype(jnp.int32)
        pcnt[slot] = lax.fori_loop(0, npages, body, 0)

    slot = g % 2

    @pl.when(g == 0)
    def _():
        buf[...] = jnp.zeros(buf.shape, F32)
        for s_ in range(2):
            for k in range(maxp):
                plist[s_, k] = 0
        issue(g, slot)

    @pl.when(g + 1 < n_steps)
    def _():
        issue(g + 1, 1 - slot)

    cnt = pcnt[slot]

    def wait_body(k, cy):
        page_copy(0, k, slot).wait()
        return cy

    lax.fori_loop(0, cnt, wait_body, 0)

    qp[...] = (q_ref[0] * scale).astype(BF16)
    rows = (_iota((r, ts), 1) == _iota((r, ts), 0) // nh).astype(BF16)
    selr[...] = _dot(rows, sel_ref[0].astype(BF16)).astype(BF16)
    _flash_init(m_ref, l_ref, acc_ref)
    q = qp[...]

    def tile(i, cy):
        masks, biases, starts = [], [], []
        for jj in range(ptile):
            k = i * ptile + jj
            j = plist[slot, k]
            base = pl.multiple_of((2 * j // LANE) * LANE, LANE)
            expand = (_iota((LANE, PAGE_SIZE), 0) == (2 * j) % LANE + _iota((LANE, PAGE_SIZE), 1) // SLC_B)
            hit = _dot(selr[:, pl.ds(base, LANE)], expand.astype(BF16))
            masks.append(jnp.where(k < cnt, hit, 0.0) > 0.5)
            biases.append(jnp.where(j == npages - 1, blast_ref[...], bfar_ref[...]) + jnp.zeros((r, PAGE_SIZE), F32))
            starts.append(pl.multiple_of(k * PAGE_SIZE, PAGE_SIZE))
        k_t = _lane_cat(buf, slot, starts, hd).astype(BF16)
        v_t = _lane_cat(buf, slot, [s + hd for s in starts], hd).astype(BF16)
        s = _dot(q, k_t) + jnp.concatenate(biases, axis=1)
        _flash_update(s, v_t, m_ref, l_ref, acc_ref, 0, jnp.concatenate(masks, axis=1), v_transposed=True)
        return cy

    lax.fori_loop(0, (cnt + ptile - 1) // ptile, tile, 0)

    qi = _iota((r, NEW_PAD), 0) // nh
    kj = _iota((r, NEW_PAD), 1)
    new_mask = (kj <= qi) & (kj < ts)
    kn = slcn_ref[0][:, :hd].astype(BF16)
    vn = slcn_ref[0][:, hd:].astype(BF16)
    _flash_update(_dot_nt(q, kn) + bnew_ref[...], vn, m_ref, l_ref, acc_ref, 0, new_mask)
    kw = win_ref[0, 0, :hd, :].astype(BF16)
    vw = win_ref[0, 0, hd:, :].astype(BF16)
    dist = wb + _iota((r, wb), 0) // nh - _iota((r, wb), 1)
    _flash_update(_dot(q, kw) + bwin_ref[:, :wb], vw, m_ref, l_ref, acc_ref, 1, dist < WINDOW, v_transposed=True)
    kn = winn_ref[0][:, :hd].astype(BF16)
    vn = winn_ref[0][:, hd:].astype(BF16)
    _flash_update(_dot_nt(q, kn) + bwin_ref[:, wb:], vn, m_ref, l_ref, acc_ref, 1, new_mask)
    gt = jax.nn.sigmoid(gate_ref[0])
    o = (gt[:, 0:1] * oc_ref[0][:, hd:] + gt[:, 1:2] * _flash_out(l_ref, acc_ref, 0)
         + gt[:, 2:3] * _flash_out(l_ref, acc_ref, 1))
    o_ref[0] = o.astype(o_ref.dtype)


def nsa_main_sample(layer, page_table, nq, gates, o_c, sel, slc_new, win_t, win_new, b_last, b_new, b_win, b_far,
                    pool, ts):
    bs, r, _ = nq.shape
    nh = r // ts
    npages = page_table.shape[1]
    wb = win_t.shape[-1]
    nsp = sel.shape[-1]
    ptile = 4 if npages % 4 == 0 else 1
    maxp = min(_pad_to(ts * min(N_SEL, nsp), ptile), npages)
    per_page = PAGE_SIZE // SLC_B
    flags = (jnp.max(sel[:, :, :npages * per_page].reshape(bs, ts, npages, per_page), axis=(1, 3)) > 0.5)
    kern = functools.partial(_nsa_main_sample_kernel, layer=layer, nh=nh, ts=ts, npages=npages, maxp=maxp,
                             ptile=ptile, n_steps=bs, past_len=npages * PAGE_SIZE, wb=wb)
    per_b = lambda w: pl.BlockSpec((1,) + w, lambda g, pt, fl: (g, 0, 0))
    const = lambda a: pl.BlockSpec(a.shape, lambda g, pt, fl: (0, 0))
    scratch = [pltpu.VMEM((2, maxp * PAGE_SIZE, LANE), F32), pltpu.SemaphoreType.DMA((2,)),
               pltpu.SMEM((2, maxp), jnp.int32), pltpu.SMEM((2,), jnp.int32),
               pltpu.VMEM((r, NSA_HD), BF16), pltpu.VMEM((r, nsp), BF16),
               pltpu.VMEM((2, r, 1), F32), pltpu.VMEM((2, r, 1), F32), pltpu.VMEM((2, r, NSA_HD), F32)]
    grid_spec = pltpu.PrefetchScalarGridSpec(
        num_scalar_prefetch=2, grid=(bs,),
        in_specs=[per_b((r, NSA_HD)), per_b((r, N_BRANCH)), per_b((r, LANE)), per_b((ts, nsp)),
                  per_b((NEW_PAD, LANE)),
                  pl.BlockSpec((1, 1, LANE, wb), lambda g, pt, fl: (layer, g, 0, 0)),
                  per_b((NEW_PAD, LANE)),
                  const(b_last), const(b_new), const(b_win), const(b_far),
                  pl.BlockSpec(memory_space=pl.ANY)],
        out_specs=per_b((r, NSA_HD)),
        scratch_shapes=scratch)
    return pl.pallas_call(
        kern, grid_spec=grid_spec, out_shape=jax.ShapeDtypeStruct((bs, r, NSA_HD), BF16),
        compiler_params=_cparams(("arbitrary",)), name="nsa_main_sample",
    )(page_table, flags.astype(jnp.int32), nq, gates, o_c, sel, _pad_new(slc_new), win_t, _pad_new(win_new),
      b_last, b_new, b_win, b_far, pool)


def _conv_prompt_kernel(g_ref, halo_ref, u_ref, w_ref, b_ref, o_ref, ext):
    ti = pl.program_id(1)
    tt = g_ref.shape[1]
    ext[0:8, :] = jnp.where(ti == 0, 0.0, halo_ref[0])
    ext[8:, :] = g_ref[0]
    c = (b_ref[...] + ext[pl.ds(8, tt), :] * w_ref[2:3, :] + ext[pl.ds(7, tt), :] * w_ref[1:2, :]
         + ext[pl.ds(6, tt), :] * w_ref[0:1, :])
    o_ref[0] = (c * jax.nn.sigmoid(c) * u_ref[0]).astype(o_ref.dtype)


def conv_act_prompt(g, u, w_conv, b_conv):
    b, t, f = g.shape
    tt = _row_tile(t, 512)
    tf = max(x for x in range(LANE, min(f, 2048) + 1, LANE) if f % x == 0)
    hb = tt // 8
    blk = pl.BlockSpec((1, tt, tf), lambda i, j, k: (i, j, k))
    vec = lambda n: pl.BlockSpec((n, tf), lambda i, j, k: (0, k))
    return pl.pallas_call(
        _conv_prompt_kernel,
        grid=(b, t // tt, f // tf),
        in_specs=[blk, pl.BlockSpec((1, 8, tf), lambda i, j, k: (i, jnp.maximum(j * hb - 1, 0), k)),
                  blk, vec(CONV_W), vec(1)],
        out_specs=blk,
        out_shape=jax.ShapeDtypeStruct((b, t, f), BF16),
        scratch_shapes=[pltpu.VMEM((tt + 8, tf), F32)],
        compiler_params=_cparams(("parallel", "parallel", "parallel")),
        name="conv_act_prompt",
    )(g, g, u, w_conv, b_conv.reshape(1, f))


def _conv_rows_kernel(g0_ref, g1_ref, g2_ref, u_ref, w_ref, b_ref, o_ref):
    c = b_ref[...] + g0_ref[...] * w_ref[0:1, :] + g1_ref[...] * w_ref[1:2, :] + g2_ref[...] * w_ref[2:3, :]
    o_ref[...] = (c * jax.nn.sigmoid(c) * u_ref[...]).astype(o_ref.dtype)


def conv_act_rows(g0, g1, g2, u, w_conv, b_conv):
    n, f = u.shape
    tf = max(x for x in range(LANE, min(f, 2048) + 1, LANE) if f % x == 0)
    blk = pl.BlockSpec((n, tf), lambda k: (0, k))
    vec = lambda m: pl.BlockSpec((m, tf), lambda k: (0, k))
    return pl.pallas_call(
        _conv_rows_kernel,
        grid=(f // tf,),
        in_specs=[blk, blk, blk, blk, vec(CONV_W), vec(1)],
        out_specs=blk,
        out_shape=jax.ShapeDtypeStruct((n, f), BF16),
        compiler_params=_cparams(("parallel",)),
        name="conv_act_rows",
    )(g0, g1, g2, u, w_conv, b_conv.reshape(1, f))


def _tflash_init(m_scr, l_scr, acc_scr):
    m_scr[...] = jnp.full(m_scr.shape, NEG, F32)
    l_scr[...] = jnp.zeros(l_scr.shape, F32)
    acc_scr[...] = jnp.zeros(acc_scr.shape, F32)


def _tflash_heads(st_scr, pt_scr, m_scr, l_scr, a_scr, slot, nh, tq, bias_fn=None, mask=None):
    for h in range(nh):
        hs = slice(h * tq, (h + 1) * tq)
        st = st_scr[:, hs]
        if bias_fn is not None:
            st = st + bias_fn(h)
        if mask is not None:
            st = jnp.where(mask, st, NEG)
        m_old = m_scr[slot, :, hs]
        m_new = jnp.maximum(m_old, jnp.max(st, axis=0, keepdims=True))
        alpha = jnp.exp(m_old - m_new)
        p = jnp.exp(st - m_new)
        l_scr[slot, :, hs] = alpha * l_scr[slot, :, hs] + jnp.sum(p, axis=0, keepdims=True)
        m_scr[slot, :, hs] = m_new
        a_scr[:, hs] = alpha
        pt_scr[:, hs] = p.astype(BF16)


def _tflash_acc(acc_scr, slot, a_scr, v_t, pt_scr):
    acc_scr[slot] = acc_scr[slot] * a_scr[...] + _dot(v_t, pt_scr[...])


def _tflash_out(l_scr, acc_scr, slot, h, tq):
    hs = slice(h * tq, (h + 1) * tq)
    l = l_scr[slot, :, hs]
    on = (acc_scr[slot, :, hs] * (1.0 / jnp.where(l == 0.0, 1.0, l))).astype(BF16)
    eye = (_iota((tq, tq), 0) == _iota((tq, tq), 1)).astype(BF16)
    return _dot_nt(eye, on)


def _transpose_cols(x_bf16, col0, ncols):
    w = x_bf16.shape[1]
    sel = (_iota((ncols, w), 1) == _iota((ncols, w), 0) + col0).astype(BF16)
    return _dot_nt(sel, x_bf16)


def _fox_prompt_t_kernel(q_ref, kv_ref, logf_ref, o_ref, kcat, v_t, cum, qs, st_scr, pt_scr,
                         m_scr, l_scr, a_scr, acc_scr, *, nh, t, tk):
    qi = pl.program_id(1)
    tq = TQ
    hp = _pad_to(nh, 16)
    one0 = FOX_HD + 3 * hp
    scale = FOX_HD ** -0.5

    @pl.when(qi == 0)
    def _build():
        low = (_iota((tk, tk), 1) <= _iota((tk, tk), 0)).astype(BF16)
        lane = _iota((tk, LANE), 1)
        ones = jnp.where((lane >= one0) & (lane < one0 + 3), 1.0, 0.0)

        def body(i, carry):
            r0 = pl.multiple_of(i * tk, tk)
            c = _sel_dot(low, logf_ref[0, pl.ds(r0, tk), :]) + carry
            cum[pl.ds(r0, tk), :] = c
            kvb = kv_ref[0, pl.ds(r0, tk), :].astype(BF16)
            keep_k = ((_iota((LANE, LANE), 0) == _iota((LANE, LANE), 1))
                      & (_iota((LANE, LANE), 0) < FOX_HD)).astype(BF16)
            kc = _dot(kvb, keep_k) + ones
            for j, part in enumerate(_split3(-c)):
                kc = kc + _dot(part, _place(nh, LANE, FOX_HD + hp * j))
            kcat[pl.ds(r0, tk), :] = kc.astype(BF16)
            v_t[:, pl.ds(r0, tk)] = _transpose_cols(kvb, FOX_HD, FOX_HD).astype(BF16)
            return c[tk - 1:tk, :]

        lax.fori_loop(0, t // tk, body, jnp.zeros((1, nh), F32))

    q = q_ref[0]
    lane = _iota((tq, LANE), 1)
    cqs = _split3(cum[pl.ds(pl.multiple_of(qi * tq, tq), tq), :])
    for h in range(nh):
        qh = _head_block(q, h, FOX_HD) * scale
        qh = jnp.where((lane >= FOX_HD) & (lane < one0) & ((lane - FOX_HD) % hp == h), 1.0, qh)
        for j in range(3):
            qh = jnp.where(lane == one0 + j, cqs[j][:, h:h + 1].astype(F32), qh)
        qs[pl.ds(h * tq, tq), :] = qh.astype(BF16)
    _tflash_init(m_scr, l_scr, acc_scr)
    qpos = qi * tq + _iota((tk, tq), 1)

    def step(kt, masked):
        r0 = pl.multiple_of(kt * tk, tk)
        st_scr[...] = _dot_nt(kcat[pl.ds(r0, tk), :], qs[...])
        mask = (r0 + _iota((tk, tq), 0) <= qpos) if masked else None
        _tflash_heads(st_scr, pt_scr, m_scr, l_scr, a_scr, 0, nh, tq, mask=mask)
        _tflash_acc(acc_scr, 0, a_scr, v_t[:, pl.ds(r0, tk)], pt_scr)

    n_kt = ((qi + 1) * tq + tk - 1) // tk

    def loop_body(kt, carry):
        step(kt, False)
        return carry

    lax.fori_loop(0, n_kt - 1, loop_body, 0)
    step(n_kt - 1, True)
    per = LANE // FOX_HD
    for h0 in range(0, nh, per):
        o = jnp.concatenate([_tflash_out(l_scr, acc_scr, 0, h0 + u, tq) for u in range(per)], axis=1)
        o_ref[0, :, h0 * FOX_HD:(h0 + per) * FOX_HD] = o.astype(o_ref.dtype)


def fox_prompt_t(fq, fkv, logf):
    b, t, _ = fq.shape
    nh = logf.shape[-1]
    tk = min(256, t)
    r = nh * TQ
    assert FOX_HD + 3 * _pad_to(nh, 16) + 3 <= LANE
    return pl.pallas_call(
        functools.partial(_fox_prompt_t_kernel, nh=nh, t=t, tk=tk),
        grid=(b, t // TQ),
        in_specs=[pl.BlockSpec((1, TQ, nh * FOX_HD), lambda i, j: (i, j, 0)),
                  pl.BlockSpec((1, t, 2 * FOX_HD), lambda i, j: (i, 0, 0)),
                  pl.BlockSpec((1, t, nh), lambda i, j: (i, 0, 0))],
        out_specs=pl.BlockSpec((1, TQ, nh * FOX_HD), lambda i, j: (i, j, 0)),
        out_shape=jax.ShapeDtypeStruct((b, t, nh * FOX_HD), BF16),
        scratch_shapes=[pltpu.VMEM((t, LANE), BF16), pltpu.VMEM((FOX_HD, t), BF16), pltpu.VMEM((t, nh), F32),
                        pltpu.VMEM((r, LANE), BF16), pltpu.VMEM((tk, r), F32), pltpu.VMEM((tk, r), BF16),
                        pltpu.VMEM((1, 1, r), F32), pltpu.VMEM((1, 1, r), F32), pltpu.VMEM((1, r), F32),
                        pltpu.VMEM((1, FOX_HD, r), F32)],
        compiler_params=_cparams(("arbitrary", "arbitrary")),
        name="fox_prompt",
    )(fq, fkv, logf)


def _mla_prompt_t_kernel(ql_ref, qr_ref, ckv_ref, kr_ref, o_ref, kcat, v_t, qs, st_scr, pt_scr,
                         m_scr, l_scr, a_scr, acc_scr, *, nh, c, t, tk):
    qi = pl.program_id(1)
    tq = TQ
    scale = (QK_NOPE + QK_ROPE) ** -0.5

    @pl.when(qi == 0)
    def _build():
        def body(i, cy):
            r0 = pl.multiple_of(i * tk, tk)
            ckvb = ckv_ref[0, pl.ds(r0, tk), :].astype(BF16)
            kcat[pl.ds(r0, tk), :c] = ckvb
            kcat[pl.ds(r0, tk), c:] = _dot(kr_ref[0, pl.ds(r0, tk), :].astype(BF16),
                                           _place(QK_ROPE, LANE, 0)).astype(BF16)
            v_t[:, pl.ds(r0, tk)] = _transpose_cols(ckvb, 0, c).astype(BF16)
            return cy
        lax.fori_loop(0, t // tk, body, 0)

    qr = (qr_ref[0] * scale).astype(BF16)
    for h in range(nh):
        sel = (_iota((nh * QK_ROPE, LANE), 0) == _iota((nh * QK_ROPE, LANE), 1) + h * QK_ROPE).astype(BF16)
        qs[pl.ds(h * tq, tq), :c] = (ql_ref[0, :, h * c:(h + 1) * c] * scale).astype(BF16)
        qs[pl.ds(h * tq, tq), c:] = _dot(qr, sel).astype(BF16)
    _tflash_init(m_scr, l_scr, acc_scr)
    qpos = qi * tq + _iota((tk, tq), 1)

    def step(kt, masked):
        r0 = pl.multiple_of(kt * tk, tk)
        st_scr[...] = _dot_nt(kcat[pl.ds(r0, tk), :], qs[...])
        mask = (r0 + _iota((tk, tq), 0) <= qpos) if masked else None
        _tflash_heads(st_scr, pt_scr, m_scr, l_scr, a_scr, 0, nh, tq, mask=mask)
        _tflash_acc(acc_scr, 0, a_scr, v_t[:, pl.ds(r0, tk)], pt_scr)

    n_kt = ((qi + 1) * tq + tk - 1) // tk

    def loop_body(kt, carry):
        step(kt, False)
        return carry

    lax.fori_loop(0, n_kt - 1, loop_body, 0)
    step(n_kt - 1, True)
    for h in range(nh):
        o_ref[0, :, h * c:(h + 1) * c] = _tflash_out(l_scr, acc_scr, 0, h, tq).astype(o_ref.dtype)


def mla_prompt_t(q_lat, q_rope, ckv, krope):
    b, t, c = ckv.shape
    nh = q_lat.shape[-1] // c
    tk = min(256, t)
    r = nh * TQ
    return pl.pallas_call(
        functools.partial(_mla_prompt_t_kernel, nh=nh, c=c, t=t, tk=tk),
        grid=(b, t // TQ),
        in_specs=[pl.BlockSpec((1, TQ, nh * c), lambda i, j: (i, j, 0)),
                  pl.BlockSpec((1, TQ, nh * QK_ROPE), lambda i, j: (i, j, 0)),
                  pl.BlockSpec((1, t, c), lambda i, j: (i, 0, 0)),
                  pl.BlockSpec((1, t, QK_ROPE), lambda i, j: (i, 0, 0))],
        out_specs=pl.BlockSpec((1, TQ, nh * c), lambda i, j: (i, j, 0)),
        out_shape=jax.ShapeDtypeStruct((b, t, nh * c), BF16),
        scratch_shapes=[pltpu.VMEM((t, c + LANE), BF16), pltpu.VMEM((c, t), BF16),
                        pltpu.VMEM((r, c + LANE), BF16), pltpu.VMEM((tk, r), F32), pltpu.VMEM((tk, r), BF16),
                        pltpu.VMEM((1, 1, r), F32), pltpu.VMEM((1, 1, r), F32), pltpu.VMEM((1, r), F32),
                        pltpu.VMEM((1, c, r), F32)],
        compiler_params=_cparams(("arbitrary", "arbitrary")),
        name="mla_prompt",
    )(q_lat, q_rope, ckv, krope)


def _nsa_main_prompt_t_kernel(q_ref, gate_ref, oc_ref, selt_ref, slc_ref, win_ref, bt_ref, o_ref,
                              slck, slcv_t, wink, winv_t, qs, selm, st_scr, pt_scr,
                              m_scr, l_scr, a_scr, acc_scr, *, nh, t, nsp):
    qi = pl.program_id(1)
    tq = tk = TQ
    hd = NSA_HD
    scale = hd ** -0.5

    @pl.when(qi == 0)
    def _build():
        def body(i, cy):
            r0 = pl.multiple_of(i * tk, tk)
            sb = slc_ref[0, pl.ds(r0, tk), :].astype(BF16)
            wb_ = win_ref[0, pl.ds(r0, tk), :].astype(BF16)
            slck[pl.ds(r0, tk), :] = sb
            wink[pl.ds(r0, tk), :] = wb_
            slcv_t[:, pl.ds(r0, tk)] = _transpose_cols(sb, hd, hd).astype(BF16)
            winv_t[:, pl.ds(r0, tk)] = _transpose_cols(wb_, hd, hd).astype(BF16)
            return cy
        lax.fori_loop(0, t // tk, body, 0)

    q = q_ref[0]
    for h in range(nh):
        qs[pl.ds(h * tq, tq), :] = (_head_block(q, h, hd) * scale).astype(BF16)
    expand_t = (_iota((t, nsp), 0) // SLC_B == _iota((t, nsp), 1)).astype(BF16)
    selm[...] = _dot(expand_t, selt_ref[...].astype(BF16))
    _tflash_init(m_scr, l_scr, acc_scr)
    dq = _iota((tk, tq), 1) - _iota((tk, tq), 0)

    def step(kt, cy):
        r0 = pl.multiple_of(kt * tk, tk)
        dist = (qi - kt) * tk + dq
        b0 = pl.multiple_of(jnp.minimum(qi - kt, 2) * tk, tk)
        bias = lambda h: bt_ref[h, pl.ds(b0, tk), :]
        st_scr[...] = _dot_nt(slck[pl.ds(r0, tk), :], qs[...])
        mask_s = jnp.where(dist >= 0, selm[pl.ds(r0, tk), :], 0.0) > 0.5
        _tflash_heads(st_scr, pt_scr, m_scr, l_scr, a_scr, 0, nh, tq, bias, mask_s)
        _tflash_acc(acc_scr, 0, a_scr, slcv_t[:, pl.ds(r0, tk)], pt_scr)

        @pl.when((qi - kt) * tk < WINDOW + tq)
        def _win():
            st_scr[...] = _dot_nt(wink[pl.ds(r0, tk), :], qs[...])
            mask_w = jnp.where(dist >= 0, dist, WINDOW) < WINDOW
            _tflash_heads(st_scr, pt_scr, m_scr, l_scr, a_scr, 1, nh, tq, bias, mask_w)
            _tflash_acc(acc_scr, 1, a_scr, winv_t[:, pl.ds(r0, tk)], pt_scr)
        return cy

    lax.fori_loop(0, qi + 1, step, 0)
    g = jax.nn.sigmoid(gate_ref[0])
    per = LANE // hd
    for h0 in range(0, nh, per):
        outs = []
        for h in range(h0, h0 + per):
            outs.append(g[:, 3 * h:3 * h + 1] * oc_ref[0, :, h * LANE + hd:(h + 1) * LANE]
                        + g[:, 3 * h + 1:3 * h + 2] * _tflash_out(l_scr, acc_scr, 0, h, tq)
                        + g[:, 3 * h + 2:3 * h + 3] * _tflash_out(l_scr, acc_scr, 1, h, tq))
        o_ref[0, :, h0 * hd:(h0 + per) * hd] = jnp.concatenate(outs, axis=1).astype(o_ref.dtype)


def nsa_main_prompt_t(nq, gates, o_c, sel_t, nslc, nwin, btiles_t):
    b, t, _ = nq.shape
    nh = btiles_t.shape[0]
    nsp = sel_t.shape[0]
    r = nh * TQ
    nq_t = t // TQ
    qblk = lambda w: pl.BlockSpec((1, TQ, w), lambda i, j: (i, j, 0))
    full = lambda w: pl.BlockSpec((1, t, w), lambda i, j: (i, 0, 0))
    return pl.pallas_call(
        functools.partial(_nsa_main_prompt_t_kernel, nh=nh, t=t, nsp=nsp),
        grid=(b, nq_t),
        in_specs=[qblk(nh * NSA_HD), qblk(LANE), qblk(nh * LANE),
                  pl.BlockSpec((nsp, TQ), lambda i, j: (0, i * nq_t + j)), full(LANE), full(LANE),
                  pl.BlockSpec(btiles_t.shape, lambda i, j: (0, 0, 0))],
        out_specs=qblk(nh * NSA_HD),
        out_shape=jax.ShapeDtypeStruct((b, t, nh * NSA_HD), BF16),
        scratch_shapes=[pltpu.VMEM((t, LANE), BF16), pltpu.VMEM((NSA_HD, t), BF16),
                        pltpu.VMEM((t, LANE), BF16), pltpu.VMEM((NSA_HD, t), BF16),
                        pltpu.VMEM((r, LANE), BF16), pltpu.VMEM((t, TQ), F32),
                        pltpu.VMEM((TQ, r), F32), pltpu.VMEM((TQ, r), BF16),
                        pltpu.VMEM((2, 1, r), F32), pltpu.VMEM((2, 1, r), F32), pltpu.VMEM((1, r), F32),
                        pltpu.VMEM((2, NSA_HD, r), F32)],
        compiler_params=_cparams(("arbitrary", "arbitrary")),
        name="nsa_main_prompt",
    )(nq, gates, o_c, sel_t, nslc, nwin, btiles_t)


def kernel(x_prompt, x_sample, cache_fox_kv, cache_fox_logf, cache_mla_ckv, cache_mla_krope, cache_nsa_cmp_kv, cache_nsa_slc_kv, state_nsa_win_kv, state_conv, page_table, g_attn, w_in, b_fgate, g_qa, wq_b, g_kva, wkv_b, w_cmp, rel_bias_table, w_out, g_ffn, w_gate, w_up, w_conv, b_conv, w_down, g_final):
    b, t, d = x_prompt.shape
    bs, ts, _ = x_sample.shape
    depth = w_in.shape[0]
    nh = d // 256
    q_lora, kv_lora = g_qa.shape[1], g_kva.shape[1]
    past_len = page_table.shape[1] * PAGE_SIZE
    n_p, n_s = b * t, bs * ts
    lay = _in_layout(nh, q_lora, kv_lora)
    pos_p = jnp.arange(t, dtype=jnp.int32)
    pos_s = past_len + jnp.arange(ts, dtype=jnp.int32)
    pos_all = jnp.concatenate([jnp.tile(pos_p, b), jnp.tile(pos_s, bs)])
    cos_k, sin_k = _rope_tables(pos_all, LANE // QK_ROPE)
    cos_q, sin_q = _rope_tables(pos_all, nh)
    x = jnp.concatenate([x_prompt.reshape(n_p, d), x_sample.reshape(n_s, d)], axis=0)
    w_buf = state_nsa_win_kv.shape[2]
    f_ff = w_gate.shape[2]
    assert t % TQ == 0 and ts < CMP_D and ts <= NEW_PAD

    ratio = CMP_L // CMP_D
    nc_p = t // CMP_D - ratio + 1
    ncp_p = _pad_to(nc_p, LANE)
    ns_p = -(-t // SLC_B)
    nsp_p = _pad_to(ns_p, LANE)
    t_s = past_len + ts
    nc_s = t_s // CMP_D - ratio + 1
    ncp_s = (past_len // CMP_D)
    ns_s = -(-t_s // SLC_B)
    nsp_s = _pad_to(ns_s, LANE)
    tk_s = PAGE_SIZE
    ar = np.arange
    cmp_end = lambda n: ar(n) * CMP_D + CMP_L - 1
    d_tile = ar(TQ)[None, :] - ar(TQ)[:, None]
    bk_tiles = np.concatenate([_bucket_np(d_tile), _bucket_np(TQ + d_tile),
                               np.full((TQ, TQ), NUM_BUCKETS - 1, np.int32)], axis=0)
    bk_cmp_p = _bucket_np(ar(t)[:, None] - cmp_end(ncp_p)[None, :])
    ps = past_len + ar(ts)[:, None]
    col_w = ar(w_buf + NEW_PAD)[None, :]
    kpos_w = np.where(col_w < w_buf, past_len - w_buf + col_w, past_len + col_w - w_buf)
    bk_s = np.concatenate([_bucket_np(ps - cmp_end(ncp_s)[None, :]),
                           _bucket_np(ps - (past_len - tk_s + ar(tk_s))[None, :]),
                           _bucket_np(ar(ts)[:, None] - ar(NEW_PAD)[None, :]),
                           _bucket_np(ps - kpos_w)], axis=1)
    btiles = bias_lookup(rel_bias_table, bk_tiles)
    bias_c_p = bias_lookup(rel_bias_table, bk_cmp_p)
    bias_s = jnp.transpose(bias_lookup(rel_bias_table, bk_s), (1, 0, 2)).reshape(ts * nh, -1)
    cuts = np.cumsum([ncp_s, tk_s, NEW_PAD])
    bias_c_s, b_last, b_new, b_win = (bias_s[:, :cuts[0]], bias_s[:, cuts[0]:cuts[1]],
                                      bias_s[:, cuts[1]:cuts[2]], bias_s[:, cuts[2]:])
    b_far = jnp.tile(rel_bias_table[NUM_BUCKETS - 1].astype(F32), ts).reshape(ts * nh, 1)
    overlap_p = jnp.asarray(_overlap_np(ncp_p, nsp_p, nc_p, ns_p), BF16)
    overlap_s = jnp.asarray(_overlap_np(ncp_s, nsp_s, nc_s, ns_s), BF16)
    posr_p = jnp.tile(pos_p, b)
    posr_s = jnp.tile(pos_s, bs)

    pool_shape = cache_fox_kv.shape[:3]
    kv_t = lambda a: jnp.transpose(a, (0, 1, 3, 4, 2)).reshape(a.shape[:2] + (a.shape[3] * a.shape[4], a.shape[2]))
    fox_pool, cmp_pool, slc_pool = kv_t(cache_fox_kv), kv_t(cache_nsa_cmp_kv), kv_t(cache_nsa_slc_kv)
    win_t = kv_t(state_nsa_win_kv)
    logf_pool = jnp.transpose(cache_fox_logf, (0, 1, 3, 2))
    kr_pool = jnp.transpose(cache_mla_krope, (0, 1, 3, 2))

    states = []
    for l in range(depth):
        hn = rmsnorm_rows(x, g_attn[l], BF16)
        p = matmul(hn, _pad_w_in(w_in[l], lay))
        logf, cqn, ckv, krope = post_proj(p, b_fgate[l], g_qa[l], g_kva[l], cos_k, sin_k, lay)
        wq = jnp.concatenate([wq_b[l][:, :, :QK_NOPE].reshape(q_lora, -1),
                              wq_b[l][:, :, QK_NOPE:].reshape(q_lora, -1)], axis=1).astype(BF16)
        q_mla = matmul(cqn, wq)
        q_rope = q_rope_rows(q_mla, nh * QK_NOPE, cos_q, sin_q)
        w_uk = jnp.transpose(wkv_b[l][:, :, :QK_NOPE], (1, 2, 0)).astype(BF16)
        w_uv = jnp.transpose(wkv_b[l][:, :, QK_NOPE:], (1, 0, 2)).astype(BF16)
        q_lat = head_matmul(q_mla, w_uk, F32)
        seg = lambda name, width=None: p[:, lay[name][0]:lay[name][0] + (width or lay[name][1])]
        fq, fkv, nq = seg("fq"), seg("fkv"), seg("nq")
        ncmp, nslc, nwin = seg("ncmp"), seg("nslc"), seg("nwin")
        wcmp = jnp.concatenate([w_cmp[l][0], w_cmp[l][1]], axis=1).astype(F32)
        pr = lambda a: a[:n_p].reshape(b, t, -1)
        sm = lambda a: a[n_p:].reshape(bs, ts, -1)
        smh = lambda a, w: a[n_p:].reshape(bs, ts * nh, w)

        o_fox_p = fox_prompt_t(pr(fq), pr(fkv), pr(logf))
        o_lat_p = mla_prompt_t(pr(q_lat), pr(q_rope), pr(ckv), pr(krope))
        oc_p, imp_p = nsa_cmp_prompt(pr(nq), pr(ncmp), wcmp, bias_c_p, overlap_p, nc_p)
        sel_p = topk_blocks(imp_p.reshape(n_p, nsp_p), posr_p, ns_p)
        o_nsa_p = nsa_main_prompt_t(pr(nq), pr(seg("ngate", LANE)), oc_p, sel_p, pr(nslc), pr(nwin), btiles)

        o_fox_s = fox_sample(l, page_table, smh(fq, FOX_HD), sm(fkv), sm(logf), fox_pool, logf_pool)
        o_lat_s = mla_sample(l, page_table, smh(q_lat, kv_lora), smh(q_rope, QK_ROPE), sm(ckv), sm(krope),
                             cache_mla_ckv, kr_pool)
        reps = PAGE_SIZE // CMP_D
        wcmp_t = jnp.stack([jnp.tile(wcmp[:CMP_D].T, (1, reps)), jnp.tile(wcmp[CMP_D:].T, (1, reps))])
        oc_s, imp_s = nsa_cmp_sample(l, page_table, smh(nq, NSA_HD), wcmp_t, bias_c_s, overlap_s, cmp_pool,
                                     ts, nc_s)
        sel_s = topk_blocks(imp_s.reshape(n_s, nsp_s), posr_s, ns_s).T.reshape(bs, ts, nsp_s)
        win_all = jnp.concatenate([state_nsa_win_kv[l].reshape(bs, w_buf, -1), sm(nwin)], axis=1)
        o_nsa_s = nsa_main_sample(l, page_table, smh(nq, NSA_HD), smh(seg("ngate"), N_BRANCH), oc_s, sel_s,
                                  sm(nslc), win_t, sm(nwin), b_last, b_new, b_win, b_far, slc_pool, ts)

        o_fox = jnp.concatenate([o_fox_p.reshape(n_p, -1), o_fox_s.reshape(n_s, -1)], axis=0)
        o_lat = jnp.concatenate([o_lat_p.reshape(n_p, -1), o_lat_s.reshape(n_s, -1)], axis=0)
        o_nsa = jnp.concatenate([o_nsa_p.reshape(n_p, -1), o_nsa_s.reshape(n_s, -1)], axis=0)
        o_mla = head_matmul(o_lat, w_uv, BF16)
        mix = jnp.concatenate([o_fox, o_mla, o_nsa], axis=1)
        x = matmul(mix, w_out[l].astype(BF16), res=x)

        h2 = rmsnorm_rows(x, g_ffn[l], BF16)
        gg = matmul(h2, w_gate[l].astype(BF16))
        uu = matmul(h2, w_up[l].astype(BF16))
        gg_p, gg_s = gg[:n_p].reshape(b, t, f_ff), gg[n_p:].reshape(bs, ts, f_ff)
        act_p = conv_act_prompt(gg_p, uu[:n_p].reshape(b, t, f_ff), w_conv[l], b_conv[l])
        ext_s = jnp.concatenate([state_conv[l], gg_s], axis=1)
        act_s = conv_act_rows(*(ext_s[:, k:k + ts].reshape(n_s, f_ff) for k in range(CONV_W)),
                              uu[n_p:], w_conv[l], b_conv[l])
        act = jnp.concatenate([act_p.reshape(n_p, f_ff), act_s], axis=0)
        x = matmul(act, w_down[l].astype(BF16), res=x)

        ext_p = jnp.concatenate([jnp.zeros((b, CONV_W - 1, f_ff), F32), gg_p], axis=1)
        win_p = jnp.pad(pr(nwin), ((0, 0), (w_buf, 0), (0, 0)))[:, -w_buf:]
        kv4 = lambda a, n: a.reshape(a.shape[0], n, 2, -1)
        states.append((
            kv4(pr(fkv), t), kv4(sm(fkv), ts), pr(logf), sm(logf), pr(ckv), sm(ckv), pr(krope), sm(krope),
            kv4(pr(ncmp), t), kv4(sm(ncmp), ts), kv4(pr(nslc), t), kv4(sm(nslc), ts),
            kv4(win_p, w_buf), kv4(win_all[:, -w_buf:], w_buf),
            ext_p[:, -(CONV_W - 1):], ext_s[:, -(CONV_W - 1):]))

    y = rmsnorm_rows(x, g_final, F32)
    stacked = [jnp.stack(z) for z in zip(*states)]
    return (y[:n_p].reshape(b, t, d), y[n_p:].reshape(bs, ts, d)) + tuple(stacked)
```
